```python
import jax, jax.numpy as jnp
from jax import lax
import numpy as np

D_MODEL = 2048
BATCH = 4
SEQ = 8192
DEPTH = 2
DEC_BATCH = 4
DEC_SEQ = 2048
PAST_LEN = 128

GRID_W = 64
N_HEADS = 16
N_KV_HEADS = 4
HEAD_DIM = D_MODEL // N_HEADS
GROUP = N_HEADS // N_KV_HEADS
Q_DIM = N_HEADS * HEAD_DIM
KV_DIM = N_KV_HEADS * HEAD_DIM
AXIAL_DIM = HEAD_DIM // 2
ROPE_THETA = 10000.0
Q_BLOCK = 128
POOL_WINDOWS = (2, 4, 8, 16)
N_POOL_GROUPS = len(POOL_WINDOWS)
POOL_GROUP_DIM = D_MODEL // N_POOL_GROUPS
PEER_HEADS = 8
PEER_QUERY_DIM = 256
PEER_HALF = PEER_QUERY_DIM // 2
N_KEYS = 128
N_EXPERTS = N_KEYS * N_KEYS
PEER_TOPK = 16
PEER_CHUNK = 128
N_MIXERS = 2
N_ATTN_LAYERS = (DEPTH + 1) // 2
N_POOL_LAYERS = DEPTH // 2
EPS = 1e-6

kernel_name = "hybrid_gqa_pool_peer_encoder"


def rms_norm(x, g):
    xf = x.astype(jnp.float32)
    y = xf * lax.rsqrt(jnp.mean(xf * xf, axis=-1, keepdims=True) + EPS)
    return (y * g.astype(jnp.float32)).astype(x.dtype)


def axial_rope_tables(seq_len):
    rows = seq_len // GRID_W
    row = jnp.repeat(jnp.arange(rows, dtype=jnp.float32), GRID_W)
    col = jnp.tile(jnp.arange(GRID_W, dtype=jnp.float32), rows)
    inv = ROPE_THETA ** (-jnp.arange(0, AXIAL_DIM, 2, dtype=jnp.float32) / AXIAL_DIM)
    ang_r = row[:, None] * inv[None, :]
    ang_c = col[:, None] * inv[None, :]
    ang = jnp.concatenate([ang_r, ang_r, ang_c, ang_c], axis=-1)
    return jnp.cos(ang), jnp.sin(ang)


def apply_axial_rope(x, cos, sin):
    xf = x.astype(jnp.float32)
    a, b, c, d = jnp.split(xf, 4, axis=-1)
    rot = jnp.concatenate([-b, a, -d, c], axis=-1)
    out = xf * cos[None, :, None, :] + rot * sin[None, :, None, :]
    return out.astype(x.dtype)


def attention_mixer(h, w_qkv, q_gain, k_gain, w_o):
    B, S, _ = h.shape
    qkv = h @ w_qkv
    q = qkv[..., :Q_DIM].reshape(B, S, N_HEADS, HEAD_DIM)
    k = qkv[..., Q_DIM:Q_DIM + KV_DIM].reshape(B, S, N_KV_HEADS, HEAD_DIM)
    v = qkv[..., Q_DIM + KV_DIM:].reshape(B, S, N_KV_HEADS, HEAD_DIM)
    q = rms_norm(q, q_gain)
    k = rms_norm(k, k_gain)
    cos, sin = axial_rope_tables(S)
    q = apply_axial_rope(q, cos, sin)
    k = apply_axial_rope(k, cos, sin)
    n_blocks = S // Q_BLOCK
    qb = q.reshape(B, n_blocks, Q_BLOCK, N_KV_HEADS, GROUP, HEAD_DIM).transpose(1, 0, 2, 3, 4, 5)
    scale = HEAD_DIM ** -0.5

    def one_block(q_blk):
        s = jnp.einsum('bqkgd,bskd->bkgqs', q_blk, k, preferred_element_type=jnp.float32) * scale
        p = jax.nn.softmax(s, axis=-1)
        o = jnp.einsum('bkgqs,bskd->bqkgd', p.astype(v.dtype), v)
        return o.reshape(B, Q_BLOCK, Q_DIM)

    o = lax.map(one_block, qb)
    o = o.transpose(1, 0, 2, 3).reshape(B, S, Q_DIM)
    return o @ w_o


def pool_mixer(h, w_in, w_group, scale, w_out):
    B, S, D = h.shape
    z = h @ w_in
    zf = z.astype(jnp.float32)
    cs = jnp.concatenate([jnp.zeros((B, 1, D), jnp.float32), jnp.cumsum(zf, axis=1)], axis=1)
    t = jnp.arange(S)
    pooled = []
    for g, w in enumerate(POOL_WINDOWS):
        lo = jnp.clip(t - w // 2, 0, S)
        hi = jnp.clip(t + w - w // 2, 0, S)
        sl = slice(g * POOL_GROUP_DIM, (g + 1) * POOL_GROUP_DIM)
        csg = cs[..., sl]
        win_sum = csg[:, hi] - csg[:, lo]
        cnt = (hi - lo).astype(jnp.float32)[None, :, None]
        pooled.append(win_sum / cnt - zf[..., sl])
    p = jnp.stack(pooled, axis=2).astype(h.dtype)
    m = jnp.einsum('bsgc,gcd->bsgd', p, w_group).reshape(B, S, D)
    return (m * scale) @ w_out


def peer_mixer(h, w_query, keys1, keys2, expert_u, expert_v):
    B, S, D = h.shape
    xc = h.reshape(-1, PEER_CHUNK, D)

    def one_chunk(x):
        C = x.shape[0]
        q = (x @ w_query).reshape(C, PEER_HEADS, 2, PEER_HALF)
        s1 = jnp.einsum('chd,hnd->chn', q[:, :, 0], keys1, preferred_element_type=jnp.float32)
        s2 = jnp.einsum('chd,hnd->chn', q[:, :, 1], keys2, preferred_element_type=jnp.float32)
        a1, i1 = lax.top_k(s1, PEER_TOPK)
        a2, i2 = lax.top_k(s2, PEER_TOPK)
        cand_s = (a1[..., :, None] + a2[..., None, :]).reshape(C, PEER_HEADS, PEER_TOPK * PEER_TOPK)
        cand_i = (i1[..., :, None] * N_KEYS + i2[..., None, :]).reshape(C, PEER_HEADS, PEER_TOPK * PEER_TOPK)
        top_s, pos = lax.top_k(cand_s, PEER_TOPK)
        ids = jnp.take_along_axis(cand_i, pos, axis=-1)
        gate = jax.nn.softmax(top_s, axis=-1)
        u = expert_u[ids]
        act = jax.nn.gelu(jnp.einsum('chkd,cd->chk', u, x), approximate=False)
        wgt = (gate * act.astype(jnp.float32)).astype(x.dtype)
        v = expert_v[ids]
        return jnp.einsum('chk,chkd->cd', wgt, v)

    return lax.map(one_chunk, xc).reshape(B, S, D)


def run_trunk(x, attn_norm, attn_w_qkv, attn_q_gain, attn_k_gain, attn_w_o,
              pool_norm, pool_w_in, pool_w_group, pool_scale, pool_w_out,
              ffn_norm, peer_w_query, peer_keys1, peer_keys2, peer_u, peer_v, final_norm):
    for i in range(DEPTH):
        j = i // N_MIXERS
        if i % N_MIXERS == 0:
            x = x + attention_mixer(rms_norm(x, attn_norm[j]), attn_w_qkv[j],
                                    attn_q_gain[j], attn_k_gain[j], attn_w_o[j])
        else:
            x = x + pool_mixer(rms_norm(x, pool_norm[j]), pool_w_in[j], pool_w_group[j],
                               pool_scale[j], pool_w_out[j])
        x = x + peer_mixer(rms_norm(x, ffn_norm[i]), peer_w_query[i], peer_keys1[i],
                           peer_keys2[i], peer_u[i], peer_v[i])
    return rms_norm(x, final_norm)


def setup_inputs(seed: int = 0) -> dict:
    key = jax.random.key(seed)
    ks = jax.random.split(key, 20)
    f32 = jnp.float32
    nrm = lambda k, shape, s: jax.random.normal(k, shape, f32) * s
    gain = lambda k, shape: 1.0 + 0.02 * jax.random.normal(k, shape, f32)
    return {
        "x_prompt": jax.random.normal(ks[0], (BATCH, SEQ, D_MODEL), f32),
        "x_sample": jax.random.normal(ks[1], (DEC_BATCH, DEC_SEQ, D_MODEL), f32),
        "attn_norm": gain(ks[2], (N_ATTN_LAYERS, D_MODEL)),
        "attn_w_qkv": nrm(ks[3], (N_ATTN_LAYERS, D_MODEL, Q_DIM + 2 * KV_DIM), D_MODEL ** -0.5),
        "attn_q_gain": gain(ks[4], (N_ATTN_LAYERS, HEAD_DIM)),
        "attn_k_gain": gain(ks[5], (N_ATTN_LAYERS, HEAD_DIM)),
        "attn_w_o": nrm(ks[6], (N_ATTN_LAYERS, Q_DIM, D_MODEL), Q_DIM ** -0.5),
        "pool_norm": gain(ks[7], (N_POOL_LAYERS, D_MODEL)),
        "pool_w_in": nrm(ks[8], (N_POOL_LAYERS, D_MODEL, D_MODEL), D_MODEL ** -0.5),
        "pool_w_group": nrm(ks[9], (N_POOL_LAYERS, N_POOL_GROUPS, POOL_GROUP_DIM, POOL_GROUP_DIM), POOL_GROUP_DIM ** -0.5),
        "pool_scale": gain(ks[10], (N_POOL_LAYERS, D_MODEL)),
        "pool_w_out": nrm(ks[11], (N_POOL_LAYERS, D_MODEL, D_MODEL), D_MODEL ** -0.5),
        "ffn_norm": gain(ks[12], (DEPTH, D_MODEL)),
        "peer_w_query": nrm(ks[13], (DEPTH, D_MODEL, PEER_HEADS * PEER_QUERY_DIM), D_MODEL ** -0.5),
        "peer_keys1": nrm(ks[14], (DEPTH, PEER_HEADS, N_KEYS, PEER_HALF), PEER_HALF ** -0.5),
        "peer_keys2": nrm(ks[15], (DEPTH, PEER_HEADS, N_KEYS, PEER_HALF), PEER_HALF ** -0.5),
        "peer_u": nrm(ks[16], (DEPTH, N_EXPERTS, D_MODEL), D_MODEL ** -0.5),
        "peer_v": nrm(ks[17], (DEPTH, N_EXPERTS, D_MODEL), (PEER_HEADS * PEER_TOPK) ** -0.5 * 4.0),
        "final_norm": gain(ks[18], (D_MODEL,)),
    }


def reference(x_prompt, x_sample, attn_norm, attn_w_qkv, attn_q_gain, attn_k_gain, attn_w_o,
              pool_norm, pool_w_in, pool_w_group, pool_scale, pool_w_out,
              ffn_norm, peer_w_query, peer_keys1, peer_keys2, peer_u, peer_v, final_norm):
    y_prompt = run_trunk(x_prompt, attn_norm, attn_w_qkv, attn_q_gain, attn_k_gain, attn_w_o,
                         pool_norm, pool_w_in, pool_w_group, pool_scale, pool_w_out,
                         ffn_norm, peer_w_query, peer_keys1, peer_keys2, peer_u, peer_v, final_norm)
    y_sample = run_trunk(x_sample, attn_norm, attn_w_qkv, attn_q_gain, attn_k_gain, attn_w_o,
                         pool_norm, pool_w_in, pool_w_group, pool_scale, pool_w_out,
                         ffn_norm, peer_w_query, peer_keys1, peer_keys2, peer_u, peer_v, final_norm)
    return (y_prompt, y_sample)
```

```python
import functools

import jax
import jax.numpy as jnp
from jax import lax
from jax.experimental import pallas as pl
from jax.experimental.pallas import tpu as pltpu

F32 = jnp.float32
BF16 = jnp.bfloat16

D_MODEL = 2048
GRID_W = 64
N_HEADS = 16
N_KV_HEADS = 4
HEAD_DIM = D_MODEL // N_HEADS
GROUP = N_HEADS // N_KV_HEADS
Q_DIM = N_HEADS * HEAD_DIM
KV_DIM = N_KV_HEADS * HEAD_DIM
AXIAL_DIM = HEAD_DIM // 2
ROPE_THETA = 10000.0
POOL_WINDOWS = (2, 4, 8, 16)
POOL_GROUP_DIM = D_MODEL // len(POOL_WINDOWS)
PEER_HEADS = 8
PEER_QUERY_DIM = 256
PEER_HALF = PEER_QUERY_DIM // 2
N_KEYS = 128
N_EXPERTS = N_KEYS * N_KEYS
PEER_TOPK = 16
EPS = 1e-6

SUBLANES = 8
POOL_HALO = 8
VMEM_LIMIT = 56 << 20

TM_PROJ = 512
TQ_ATTN = 256
TK_ATTN = 512
TM_ROUTE = 256
TM_PEER = 512
E1_PER_STEP = 4
EB_PEER = E1_PER_STEP * N_KEYS


def _params(n_parallel, n_arbitrary=0):
    sem = ("parallel",) * n_parallel + ("arbitrary",) * n_arbitrary
    return pltpu.CompilerParams(dimension_semantics=sem, vmem_limit_bytes=VMEM_LIMIT)


def _const_spec(shape):
    nd = len(shape)
    return pl.BlockSpec(shape, lambda *_: (0,) * nd)


def _rms(x, g):
    return x * lax.rsqrt(jnp.mean(x * x, axis=-1, keepdims=True) + EPS) * g


def _qkv_kernel(x_ref, g_ref, w_ref, qg_ref, kg_ref, cos_ref, sin_ref, q_ref, k_ref, v_ref):
    tm = x_ref.shape[0]
    xn = _rms(x_ref[...], g_ref[...]).astype(BF16)
    cos = cos_ref[...]
    sin = sin_ref[...]
    lane = lax.broadcasted_iota(jnp.int32, (tm, HEAD_DIM), 1)
    first_quarter = (lane % AXIAL_DIM) < (AXIAL_DIM // 2)

    def head(y, gain, scale):
        y = _rms(y, gain)
        rot = jnp.where(first_quarter,
                        pltpu.roll(y, HEAD_DIM - AXIAL_DIM // 2, 1),
                        pltpu.roll(y, AXIAL_DIM // 2, 1))
        return (y * cos + rot * sin) * scale

    chunk = KV_DIM
    heads_per_chunk = chunk // HEAD_DIM
    q_scale = HEAD_DIM ** -0.5
    for c in range(Q_DIM // chunk):
        y = jnp.dot(xn, w_ref[:, c * chunk:(c + 1) * chunk], preferred_element_type=F32)
        for hh in range(heads_per_chunk):
            lo = hh * HEAD_DIM
            out = head(y[:, lo:lo + HEAD_DIM], qg_ref[...], q_scale)
            q_ref[:, c * chunk + lo:c * chunk + lo + HEAD_DIM] = out.astype(BF16)
    y = jnp.dot(xn, w_ref[:, Q_DIM:Q_DIM + KV_DIM], preferred_element_type=F32)
    for hh in range(heads_per_chunk):
        lo = hh * HEAD_DIM
        out = head(y[:, lo:lo + HEAD_DIM], kg_ref[...], 1.0)
        k_ref[:, lo:lo + HEAD_DIM] = out.astype(BF16)
    y = jnp.dot(xn, w_ref[:, Q_DIM + KV_DIM:], preferred_element_type=F32)
    v_ref[...] = y.astype(BF16)


def _qkv_proj(x, g, w, qg, kg, cos, sin, seq):
    t = x.shape[0]
    tm = TM_PROJ
    tiles_per_seq = seq // tm
    tok = lambda n: pl.BlockSpec((tm, n), lambda i: (i, 0))
    pos = pl.BlockSpec((tm, HEAD_DIM), lambda i: (i % tiles_per_seq, 0))
    return pl.pallas_call(
        _qkv_kernel,
        grid=(t // tm,),
        in_specs=[tok(D_MODEL), _const_spec((1, D_MODEL)), _const_spec(w.shape),
                  _const_spec((1, HEAD_DIM)), _const_spec((1, HEAD_DIM)), pos, pos],
        out_specs=[tok(Q_DIM), tok(KV_DIM), tok(KV_DIM)],
        out_shape=[jax.ShapeDtypeStruct((t, Q_DIM), BF16),
                   jax.ShapeDtypeStruct((t, KV_DIM), BF16),
                   jax.ShapeDtypeStruct((t, KV_DIM), BF16)],
        compiler_params=_params(1),
        name="qkv_proj",
    )(x, g, w, qg, kg, cos, sin)


def _attn_kernel(q_ref, k_ref, v_ref, o_ref):
    tq = q_ref.shape[0]
    seq = k_ref.shape[0]
    rows = GROUP * tq
    q = q_ref[...]
    qs = jnp.concatenate([q[:, h * HEAD_DIM:(h + 1) * HEAD_DIM] for h in range(GROUP)], axis=0)

    def body(c, carry):
        m, l, acc = carry
        start = pl.multiple_of(c * TK_ATTN, TK_ATTN)
        kc = k_ref[pl.ds(start, TK_ATTN), :]
        vc = v_ref[pl.ds(start, TK_ATTN), :]
        s = lax.dot_general(qs, kc, (((1,), (1,)), ((), ())), preferred_element_type=F32)
        m_new = jnp.maximum(m, jnp.max(s, axis=-1, keepdims=True))
        alpha = jnp.exp(m - m_new)
        p = jnp.exp(s - m_new)
        l = alpha * l + jnp.sum(p, axis=-1, keepdims=True)
        acc = alpha * acc + jnp.dot(p.astype(BF16), vc, preferred_element_type=F32)
        return m_new, l, acc

    init = (jnp.full((rows, 1), -jnp.inf, F32), jnp.zeros((rows, 1), F32),
            jnp.zeros((rows, HEAD_DIM), F32))
    _, l, acc = lax.fori_loop(0, seq // TK_ATTN, body, init)
    o = acc / l
    for h in range(GROUP):
        o_ref[:, h * HEAD_DIM:(h + 1) * HEAD_DIM] = o[h * tq:(h + 1) * tq].astype(BF16)


def _attention(q, k, v, batch, seq):
    t = q.shape[0]
    tq = TQ_ATTN
    q_tiles = seq // tq
    qo_spec = pl.BlockSpec((tq, GROUP * HEAD_DIM), lambda b, g, i: (b * q_tiles + i, g))
    kv_spec = pl.BlockSpec((seq, HEAD_DIM), lambda b, g, i: (b, g))
    return pl.pallas_call(
        _attn_kernel,
        grid=(batch, N_KV_HEADS, q_tiles),
        in_specs=[qo_spec, kv_spec, kv_spec],
        out_specs=qo_spec,
        out_shape=jax.ShapeDtypeStruct((t, Q_DIM), BF16),
        compiler_params=_params(3),
        name="attention",
    )(q, k, v)


def _proj_residual_kernel(a_ref, w_ref, res_ref, o_ref):
    o_ref[...] = res_ref[...] + jnp.dot(a_ref[...], w_ref[...], preferred_element_type=F32)


def _proj_residual(a, w, res):
    t, k = a.shape
    n = w.shape[1]
    tm = TM_PROJ
    return pl.pallas_call(
        _proj_residual_kernel,
        grid=(t // tm,),
        in_specs=[pl.BlockSpec((tm, k), lambda i: (i, 0)), _const_spec(w.shape),
                  pl.BlockSpec((tm, n), lambda i: (i, 0))],
        out_specs=pl.BlockSpec((tm, n), lambda i: (i, 0)),
        out_shape=jax.ShapeDtypeStruct((t, n), F32),
        compiler_params=_params(1),
        name="proj_residual",
    )(a, w, res)


def _norm_proj_kernel(x_ref, g_ref, w_ref, o_ref):
    xn = _rms(x_ref[...], g_ref[...]).astype(BF16)
    o_ref[...] = jnp.dot(xn, w_ref[...], preferred_element_type=F32)


def _norm_proj(x, g, w):
    t, k = x.shape
    n = w.shape[1]
    tm = TM_PROJ
    return pl.pallas_call(
        _norm_proj_kernel,
        grid=(t // tm,),
        in_specs=[pl.BlockSpec((tm, k), lambda i: (i, 0)), _const_spec((1, k)), _const_spec(w.shape)],
        out_specs=pl.BlockSpec((tm, n), lambda i: (i, 0)),
        out_shape=jax.ShapeDtypeStruct((t, n), F32),
        compiler_params=_params(1),
        name="norm_proj",
    )(x, g, w)


def _pool_kernel(z_ref, zprev_ref, znext_ref, wg_ref, scale_ref, wout_ref, res_ref, o_ref, *, seq):
    tm = z_ref.shape[0]
    ext_rows = tm + 2 * POOL_HALO
    tiles_per_seq = seq // tm
    ti = pl.program_id(0) % tiles_per_seq
    has_prev = ti > 0
    has_next = ti < tiles_per_seq - 1
    pos = ti * tm + lax.broadcasted_iota(jnp.int32, (tm, 1), 0)
    mixed = []
    for g, w in enumerate(POOL_WINDOWS):
        sl = slice(g * POOL_GROUP_DIM, (g + 1) * POOL_GROUP_DIM)
        zc = z_ref[:, sl]
        zp = jnp.where(has_prev, zprev_ref[:, sl], 0.0)
        zn = jnp.where(has_next, znext_ref[:, sl], 0.0)
        a = jnp.concatenate([zp, zc, zn], axis=0)
        step = 1
        while step < w:
            a = a + pltpu.roll(a, ext_rows - step, 0)
            step *= 2
        if w // 2 < POOL_HALO:
            a = pltpu.roll(a, w // 2, 0)
            win = a[POOL_HALO:POOL_HALO + tm]
        else:
            win = a[0:tm]
        lo = jnp.maximum(pos - w // 2, 0)
        hi = jnp.minimum(pos + (w - w // 2), seq)
        cnt = (hi - lo).astype(F32)
        pooled = (win / cnt - zc).astype(BF16)
        mixed.append(jnp.dot(pooled, wg_ref[g], preferred_element_type=F32))
    m = jnp.concatenate(mixed, axis=1) * scale_ref[...]
    o_ref[...] = res_ref[...] + jnp.dot(m.astype(BF16), wout_ref[...], preferred_element_type=F32)


def _pool_mix(z, wg, scale, wout, res, seq):
    t = z.shape[0]
    tm = TM_PROJ
    halo_per_tile = tm // POOL_HALO
    n_halo_blocks = t // POOL_HALO
    tok = pl.BlockSpec((tm, D_MODEL), lambda i: (i, 0))
    prev = pl.BlockSpec((POOL_HALO, D_MODEL), lambda i: (jnp.maximum(i * halo_per_tile - 1, 0), 0))
    nxt = pl.BlockSpec((POOL_HALO, D_MODEL),
                       lambda i: (jnp.minimum((i + 1) * halo_per_tile, n_halo_blocks - 1), 0))
    return pl.pallas_call(
        functools.partial(_pool_kernel, seq=seq),
        grid=(t // tm,),
        in_specs=[tok, prev, nxt, _const_spec(wg.shape), _const_spec((1, D_MODEL)),
                  _const_spec(wout.shape), tok],
        out_specs=tok,
        out_shape=jax.ShapeDtypeStruct((t, D_MODEL), F32),
        compiler_params=_params(1),
        name="pool_mix",
    )(z, z, z, wg, scale, wout, res)


def _top_rows(s, out_ref):
    cur = s
    for r in range(PEER_TOPK):
        m = jnp.max(cur, axis=0, keepdims=True)
        out_ref[r:r + 1, :] = m
        if r + 1 < PEER_TOPK:
            cur = jnp.where(cur == m, -jnp.inf, cur)


def _route_kernel(x_ref, g_ref, wqt_ref, k1_ref, k2_ref,
                  xnt_ref, s1m_ref, s2_ref, tau_ref, iz_ref,
                  qt_ref, a1_ref, a2_ref):
    xn = _rms(x_ref[...], g_ref[...])
    xnt = xn.T.astype(BF16)
    xnt_ref[...] = xnt
    qt_ref[...] = jnp.dot(wqt_ref[...], xnt, preferred_element_type=F32)

    def head(h, carry):
        base = pl.multiple_of(h * PEER_QUERY_DIM, PEER_QUERY_DIM)
        q1 = qt_ref[pl.ds(base, PEER_HALF), :].astype(BF16)
        q2 = qt_ref[pl.ds(base + PEER_HALF, PEER_HALF), :].astype(BF16)
        s1 = jnp.dot(k1_ref[h], q1, preferred_element_type=F32)
        s2 = jnp.dot(k2_ref[h], q2, preferred_element_type=F32)
        _top_rows(s1, a1_ref)
        _top_rows(s2, a2_ref)
        m = a1_ref[0:1, :] + a2_ref[0:1, :]
        s1m_ref[h] = s1 - m
        s2_ref[h] = s2
        a1m = a1_ref[...] - m
        a2 = a2_ref[...]
        half = PEER_TOPK // 2
        slabs = [a1m[0:1] + a2]
        slabs += [a1m[i:i + 1] + a2[0:half] for i in range(1, half)]
        slabs += [a1m[half:] + a2[0:1]]
        cur = jnp.concatenate(slabs, axis=0)
        z = jnp.zeros_like(m)
        c = m
        for r in range(PEER_TOPK):
            c = jnp.max(cur, axis=0, keepdims=True)
            z = z + jnp.exp(c)
            if r + 1 < PEER_TOPK:
                cur = jnp.where(cur == c, -jnp.inf, cur)
        tau_ref[pl.ds(h, 1), :] = c
        iz_ref[pl.ds(h, 1), :] = 1.0 / z
        return carry

    lax.fori_loop(0, PEER_HEADS, head, 0)


def _peer_route(x, g, wqt, k1, k2):
    t = x.shape[0]
    tm = TM_ROUTE
    score_spec = pl.BlockSpec((PEER_HEADS, N_KEYS, tm), lambda i: (0, 0, i))
    head_spec = pl.BlockSpec((PEER_HEADS, tm), lambda i: (0, i))
    return pl.pallas_call(
        _route_kernel,
        grid=(t // tm,),
        in_specs=[pl.BlockSpec((tm, D_MODEL), lambda i: (i, 0)), _const_spec((1, D_MODEL)),
                  _const_spec(wqt.shape), _const_spec(k1.shape), _const_spec(k2.shape)],
        out_specs=[pl.BlockSpec((D_MODEL, tm), lambda i: (0, i)), score_spec, score_spec,
                   head_spec, head_spec],
        out_shape=[jax.ShapeDtypeStruct((D_MODEL, t), BF16),
                   jax.ShapeDtypeStruct((PEER_HEADS, N_KEYS, t), F32),
                   jax.ShapeDtypeStruct((PEER_HEADS, N_KEYS, t), F32),
                   jax.ShapeDtypeStruct((PEER_HEADS, t), F32),
                   jax.ShapeDtypeStruct((PEER_HEADS, t), F32)],
        scratch_shapes=[pltpu.VMEM((PEER_HEADS * PEER_QUERY_DIM, tm), F32),
                        pltpu.VMEM((PEER_TOPK, tm), F32),
                        pltpu.VMEM((PEER_TOPK, tm), F32)],
        compiler_params=_params(1),
        name="peer_route",
    )(x, g, wqt, k1, k2)


def _gelu(x):
    return x * 0.5 * (1.0 + lax.erf(x * (2.0 ** -0.5)))


def _peer_kernel(xnt_ref, u_ref, vt_ref, s1m_ref, s2_ref, tau_ref, iz_ref, res_ref, fg_ref,
                 o_ref, acc_ref, *, final_norm):
    j = pl.program_id(1)

    @pl.when(j == 0)
    def _():
        acc_ref[...] = jnp.zeros_like(acc_ref)

    st = jnp.dot(u_ref[...], xnt_ref[...], preferred_element_type=F32)
    weights = []
    for kk in range(E1_PER_STEP):
        e1 = j * E1_PER_STEP + kk
        gate = jnp.zeros((N_KEYS, st.shape[1]), F32)
        for h in range(PEER_HEADS):
            d = s1m_ref[h, pl.ds(e1, 1), :] + s2_ref[h]
            sel = jnp.where(d >= tau_ref[h:h + 1, :], jnp.exp(d), 0.0)
            gate = gate + sel * iz_ref[h:h + 1, :]
        weights.append((gate * _gelu(st[kk * N_KEYS:(kk + 1) * N_KEYS])).astype(BF16))
    w = jnp.concatenate(weights, axis=0)
    acc_ref[...] += jnp.dot(vt_ref[...], w, preferred_element_type=F32)

    @pl.when(j == pl.num_programs(1) - 1)
    def _():
        y = res_ref[...] + acc_ref[...].T
        if final_norm:
            y = _rms(y, fg_ref[...])
        o_ref[...] = y


def _peer_experts(xnt, u, vt, s1m, s2, tau, iz, res, fg, final_norm):
    t = res.shape[0]
    tm = TM_PEER
    score_spec = pl.BlockSpec((PEER_HEADS, N_KEYS, tm), lambda i, j: (0, 0, i))
    head_spec = pl.BlockSpec((PEER_HEADS, tm), lambda i, j: (0, i))
    tok = pl.BlockSpec((tm, D_MODEL), lambda i, j: (i, 0))
    return pl.pallas_call(
        functools.partial(_peer_kernel, final_norm=final_norm),
        grid=(t // tm, N_EXPERTS // EB_PEER),
        in_specs=[pl.BlockSpec((D_MODEL, tm), lambda i, j: (0, i)),
                  pl.BlockSpec((EB_PEER, D_MODEL), lambda i, j: (j, 0)),
                  pl.BlockSpec((D_MODEL, EB_PEER), lambda i, j: (0, j)),
                  score_spec, score_spec, head_spec, head_spec, tok,
                  pl.BlockSpec((1, D_MODEL), lambda i, j: (0, 0))],
        out_specs=tok,
        out_shape=jax.ShapeDtypeStruct((t, D_MODEL), F32),
        scratch_shapes=[pltpu.VMEM((D_MODEL, tm), F32)],
        compiler_params=_params(1, 1),
        name="peer_experts",
    )(xnt, u, vt, s1m, s2, tau, iz, res, fg)


def _rope_tables(seq):
    rows = seq // GRID_W
    row = jnp.repeat(jnp.arange(rows, dtype=F32), GRID_W)
    col = jnp.tile(jnp.arange(GRID_W, dtype=F32), rows)
    inv = ROPE_THETA ** (-jnp.arange(0, AXIAL_DIM, 2, dtype=F32) / AXIAL_DIM)
    ang_r = row[:, None] * inv[None, :]
    ang_c = col[:, None] * inv[None, :]
    ang = jnp.concatenate([ang_r, ang_r, ang_c, ang_c], axis=-1)
    quarter = AXIAL_DIM // 2
    sign = jnp.where((jnp.arange(HEAD_DIM) % AXIAL_DIM) < quarter, -1.0, 1.0).astype(F32)
    return jnp.cos(ang), jnp.sin(ang) * sign[None, :]


def _peer(x, lw, final_gain, final_norm):
    xnt, s1m, s2, tau, iz = _peer_route(x, lw["ffn_norm"], lw["wqt"], lw["k1"], lw["k2"])
    return _peer_experts(xnt, lw["u"], lw["vt"], s1m, s2, tau, iz, x, final_gain, final_norm)


def _trunk(x3, w):
    batch, seq, _ = x3.shape
    x = x3.reshape(batch * seq, D_MODEL)
    cos, sin = _rope_tables(seq)
    q, k, v = _qkv_proj(x, w["attn_norm"], w["w_qkv"], w["q_gain"], w["k_gain"], cos, sin, seq)
    o = _attention(q, k, v, batch, seq)
    x = _proj_residual(o, w["w_o"], x)
    x = _peer(x, w["peer"][0], w["final_norm"], False)
    z = _norm_proj(x, w["pool_norm"], w["pool_w_in"])
    x = _pool_mix(z, w["pool_w_group"], w["pool_scale"], w["pool_w_out"], x, seq)
    x = _peer(x, w["peer"][1], w["final_norm"], True)
    return x.reshape(batch, seq, D_MODEL)


def kernel(x_prompt, x_sample, attn_norm, attn_w_qkv, attn_q_gain, attn_k_gain, attn_w_o,
           pool_norm, pool_w_in, pool_w_group, pool_scale, pool_w_out,
           ffn_norm, peer_w_query, peer_keys1, peer_keys2, peer_u, peer_v, final_norm):
    row = lambda a: a.reshape(1, -1)
    w = {
        "attn_norm": row(attn_norm[0]),
        "w_qkv": attn_w_qkv[0].astype(BF16),
        "q_gain": row(attn_q_gain[0]),
        "k_gain": row(attn_k_gain[0]),
        "w_o": attn_w_o[0].astype(BF16),
        "pool_norm": row(pool_norm[0]),
        "pool_w_in": pool_w_in[0].astype(BF16),
        "pool_w_group": pool_w_group[0].astype(BF16),
        "pool_scale": row(pool_scale[0]),
        "pool_w_out": pool_w_out[0].astype(BF16),
        "final_norm": row(final_norm),
        "peer": [
            {
                "ffn_norm": row(ffn_norm[i]),
                "wqt": peer_w_query[i].T.astype(BF16),
                "k1": peer_keys1[i].astype(BF16),
                "k2": peer_keys2[i].astype(BF16),
                "u": peer_u[i].astype(BF16),
                "vt": peer_v[i].T.astype(BF16),
            }
            for i in range(2)
        ],
    }
    return (_trunk(x_prompt, w), _trunk(x_sample, w))
```

```python
import functools

import jax
import jax.numpy as jnp
from jax import lax
from jax.experimental import pallas as pl
from jax.experimental.pallas import tpu as pltpu

F32 = jnp.float32
BF16 = jnp.bfloat16

D_MODEL = 2048
GRID_W = 64
N_HEADS = 16
N_KV_HEADS = 4
HEAD_DIM = D_MODEL // N_HEADS
GROUP = N_HEADS // N_KV_HEADS
Q_DIM = N_HEADS * HEAD_DIM
KV_DIM = N_KV_HEADS * HEAD_DIM
AXIAL_DIM = HEAD_DIM // 2
ROPE_THETA = 10000.0
POOL_WINDOWS = (2, 4, 8, 16)
POOL_GROUP_DIM = D_MODEL // len(POOL_WINDOWS)
PEER_HEADS = 8
PEER_QUERY_DIM = 256
PEER_HALF = PEER_QUERY_DIM // 2
N_KEYS = 128
N_EXPERTS = N_KEYS * N_KEYS
PEER_TOPK = 16
EPS = 1e-6

LOG2_E = 1.4426950408889634
LANES = 128
POOL_HALO = 8
VMEM_LIMIT = 56 << 20

TM_PROJ = 512
TQ_ATTN = 256
TK_ATTN = 2048
SUB_PEER = 256
TM_PEER = 512
E1_PER_STEP = 4
EB_PEER = E1_PER_STEP * N_KEYS
GATE_ROWS = 64
PEER_STEPS = N_EXPERTS // EB_PEER
assert PEER_STEPS % 2 == 0


def _params(n_parallel, n_arbitrary=0):
    sem = ("parallel",) * n_parallel + ("arbitrary",) * n_arbitrary
    return pltpu.CompilerParams(dimension_semantics=sem, vmem_limit_bytes=VMEM_LIMIT)


def _const_spec(shape):
    nd = len(shape)
    return pl.BlockSpec(shape, lambda *_: (0,) * nd)


def _rms(x, g):
    return x * lax.rsqrt(jnp.mean(x * x, axis=-1, keepdims=True) + EPS) * g


def _qkv_kernel(x_ref, g_ref, w_ref, qg_ref, kg_ref, cos_ref, sin_ref, q_ref, k_ref, v_ref):
    tm = x_ref.shape[0]
    xn = _rms(x_ref[...], g_ref[...]).astype(BF16)
    cos = cos_ref[...]
    sin = sin_ref[...]
    lane = lax.broadcasted_iota(jnp.int32, (tm, HEAD_DIM), 1)
    first_quarter = (lane % AXIAL_DIM) < (AXIAL_DIM // 2)

    def head(y, gain, scale):
        y = _rms(y, gain)
        rot = jnp.where(first_quarter,
                        pltpu.roll(y, HEAD_DIM - AXIAL_DIM // 2, 1),
                        pltpu.roll(y, AXIAL_DIM // 2, 1))
        return (y * cos + rot * sin) * scale

    chunk = KV_DIM
    heads_per_chunk = chunk // HEAD_DIM
    q_scale = HEAD_DIM ** -0.5 * LOG2_E
    for c in range(Q_DIM // chunk):
        y = jnp.dot(xn, w_ref[:, c * chunk:(c + 1) * chunk], preferred_element_type=F32)
        for hh in range(heads_per_chunk):
            lo = hh * HEAD_DIM
            out = head(y[:, lo:lo + HEAD_DIM], qg_ref[...], q_scale)
            q_ref[:, c * chunk + lo:c * chunk + lo + HEAD_DIM] = out.astype(BF16)
    y = jnp.dot(xn, w_ref[:, Q_DIM:Q_DIM + KV_DIM], preferred_element_type=F32)
    for hh in range(heads_per_chunk):
        lo = hh * HEAD_DIM
        out = head(y[:, lo:lo + HEAD_DIM], kg_ref[...], 1.0)
        k_ref[:, lo:lo + HEAD_DIM] = out.astype(BF16)
    y = jnp.dot(xn, w_ref[:, Q_DIM + KV_DIM:], preferred_element_type=F32)
    ones_col = (lax.broadcasted_iota(jnp.int32, (tm, HEAD_DIM), 1) == 0).astype(BF16)
    for hh in range(heads_per_chunk):
        lo = hh * HEAD_DIM
        v_ref[:, 2 * lo:2 * lo + HEAD_DIM] = y[:, lo:lo + HEAD_DIM].astype(BF16)
        v_ref[:, 2 * lo + HEAD_DIM:2 * lo + 2 * HEAD_DIM] = ones_col


def _qkv_proj(x, g, w, qg, kg, cos, sin, seq):
    t = x.shape[0]
    tm = TM_PROJ
    tiles_per_seq = seq // tm
    tok = lambda n: pl.BlockSpec((tm, n), lambda i: (i, 0))
    pos = pl.BlockSpec((tm, HEAD_DIM), lambda i: (i % tiles_per_seq, 0))
    return pl.pallas_call(
        _qkv_kernel,
        grid=(t // tm,),
        in_specs=[tok(D_MODEL), _const_spec((1, D_MODEL)), _const_spec(w.shape),
                  _const_spec((1, HEAD_DIM)), _const_spec((1, HEAD_DIM)), pos, pos],
        out_specs=[tok(Q_DIM), tok(KV_DIM), tok(2 * KV_DIM)],
        out_shape=[jax.ShapeDtypeStruct((t, Q_DIM), BF16),
                   jax.ShapeDtypeStruct((t, KV_DIM), BF16),
                   jax.ShapeDtypeStruct((t, 2 * KV_DIM), BF16)],
        compiler_params=_params(1),
        name="qkv_proj",
    )(x, g, w, qg, kg, cos, sin)


def _attn_kernel(q_ref, k_ref, v_ref, o_ref):
    tq = q_ref.shape[0]
    seq = k_ref.shape[0]
    rows = GROUP * tq
    q = q_ref[...]
    qs = jnp.concatenate([q[:, h * HEAD_DIM:(h + 1) * HEAD_DIM] for h in range(GROUP)], axis=0)

    tk = min(TK_ATTN, seq)

    half_rows = rows // 2

    def body(c, carry):
        m, acc_top, acc_bot = carry
        start = pl.multiple_of(c * tk, tk)
        kc = k_ref[pl.ds(start, tk), :]
        vc = v_ref[pl.ds(start, tk), :]
        s = lax.dot_general(qs, kc, (((1,), (1,)), ((), ())), preferred_element_type=F32)
        m_new = jnp.maximum(m, jnp.max(s, axis=-1, keepdims=True))
        alpha = jnp.exp2(m - m_new)
        p = jnp.exp2(s - m_new).astype(BF16)
        acc_top = alpha[:half_rows] * acc_top + jnp.dot(p[:half_rows], vc, preferred_element_type=F32)
        acc_bot = alpha[half_rows:] * acc_bot + jnp.dot(p[half_rows:], vc, preferred_element_type=F32)
        return m_new, acc_top, acc_bot

    zeros_half = jnp.zeros((half_rows, 2 * HEAD_DIM), F32)
    init = (jnp.full((rows, 1), -jnp.inf, F32), zeros_half, zeros_half)
    _, acc_top, acc_bot = lax.fori_loop(0, seq // tk, body, init)
    acc = jnp.concatenate([acc_top, acc_bot], axis=0)
    o = acc[:, :HEAD_DIM] / acc[:, HEAD_DIM:HEAD_DIM + 1]
    for h in range(GROUP):
        o_ref[:, h * HEAD_DIM:(h + 1) * HEAD_DIM] = o[h * tq:(h + 1) * tq].astype(BF16)


def _attention(q, k, v, batch, seq):
    t = q.shape[0]
    tq = TQ_ATTN
    q_tiles = seq // tq
    qo_spec = pl.BlockSpec((tq, GROUP * HEAD_DIM), lambda b, g, i: (b * q_tiles + i, g))
    k_spec = pl.BlockSpec((seq, HEAD_DIM), lambda b, g, i: (b, g))
    v_spec = pl.BlockSpec((seq, 2 * HEAD_DIM), lambda b, g, i: (b, g))
    return pl.pallas_call(
        _attn_kernel,
        grid=(batch, N_KV_HEADS, q_tiles),
        in_specs=[qo_spec, k_spec, v_spec],
        out_specs=qo_spec,
        out_shape=jax.ShapeDtypeStruct((t, Q_DIM), BF16),
        compiler_params=_params(3),
        name="attention",
    )(q, k, v)


def _proj_residual_kernel(a_ref, w_ref, res_ref, o_ref):
    o_ref[...] = res_ref[...] + jnp.dot(a_ref[...], w_ref[...], preferred_element_type=F32)


def _proj_residual(a, w, res):
    t, k = a.shape
    n = w.shape[1]
    tm = TM_PROJ
    return pl.pallas_call(
        _proj_residual_kernel,
        grid=(t // tm,),
        in_specs=[pl.BlockSpec((tm, k), lambda i: (i, 0)), _const_spec(w.shape),
                  pl.BlockSpec((tm, n), lambda i: (i, 0))],
        out_specs=pl.BlockSpec((tm, n), lambda i: (i, 0)),
        out_shape=jax.ShapeDtypeStruct((t, n), F32),
        compiler_params=_params(1),
        name="proj_residual",
    )(a, w, res)


def _norm_proj_kernel(x_ref, g_ref, w_ref, o_ref):
    xn = _rms(x_ref[...], g_ref[...]).astype(BF16)
    o_ref[...] = jnp.dot(xn, w_ref[...], preferred_element_type=F32)


def _norm_proj(x, g, w):
    t, k = x.shape
    n = w.shape[1]
    tm = TM_PROJ
    return pl.pallas_call(
        _norm_proj_kernel,
        grid=(t // tm,),
        in_specs=[pl.BlockSpec((tm, k), lambda i: (i, 0)), _const_spec((1, k)), _const_spec(w.shape)],
        out_specs=pl.BlockSpec((tm, n), lambda i: (i, 0)),
        out_shape=jax.ShapeDtypeStruct((t, n), F32),
        compiler_params=_params(1),
        name="norm_proj",
    )(x, g, w)


def _pool_kernel(z_ref, zprev_ref, znext_ref, wg_ref, scale_ref, wout_ref, res_ref, o_ref, *, seq):
    tm = z_ref.shape[0]
    ext_rows = tm + 2 * POOL_HALO
    tiles_per_seq = seq // tm
    ti = pl.program_id(0) % tiles_per_seq
    has_prev = ti > 0
    has_next = ti < tiles_per_seq - 1
    pos = ti * tm + lax.broadcasted_iota(jnp.int32, (tm, 1), 0)
    mixed = []
    for g, w in enumerate(POOL_WINDOWS):
        sl = slice(g * POOL_GROUP_DIM, (g + 1) * POOL_GROUP_DIM)
        zc = z_ref[:, sl]
        zp = jnp.where(has_prev, zprev_ref[:, sl], 0.0)
        zn = jnp.where(has_next, znext_ref[:, sl], 0.0)
        a = jnp.concatenate([zp, zc, zn], axis=0)
        step = 1
        while step < w:
            a = a + pltpu.roll(a, ext_rows - step, 0)
            step *= 2
        if w // 2 < POOL_HALO:
            a = pltpu.roll(a, w // 2, 0)
            win = a[POOL_HALO:POOL_HALO + tm]
        else:
            win = a[0:tm]
        lo = jnp.maximum(pos - w // 2, 0)
        hi = jnp.minimum(pos + (w - w // 2), seq)
        cnt = (hi - lo).astype(F32)
        pooled = (win / cnt - zc).astype(BF16)
        mixed.append(jnp.dot(pooled, wg_ref[g], preferred_element_type=F32))
    m = jnp.concatenate(mixed, axis=1) * scale_ref[...]
    o_ref[...] = res_ref[...] + jnp.dot(m.astype(BF16), wout_ref[...], preferred_element_type=F32)


def _pool_mix(z, wg, scale, wout, res, seq):
    t = z.shape[0]
    tm = TM_PROJ
    halo_per_tile = tm // POOL_HALO
    n_halo_blocks = t // POOL_HALO
    tok = pl.BlockSpec((tm, D_MODEL), lambda i: (i, 0))
    prev = pl.BlockSpec((POOL_HALO, D_MODEL), lambda i: (jnp.maximum(i * halo_per_tile - 1, 0), 0))
    nxt = pl.BlockSpec((POOL_HALO, D_MODEL),
                       lambda i: (jnp.minimum((i + 1) * halo_per_tile, n_halo_blocks - 1), 0))
    return pl.pallas_call(
        functools.partial(_pool_kernel, seq=seq),
        grid=(t // tm,),
        in_specs=[tok, prev, nxt, _const_spec(wg.shape), _const_spec((1, D_MODEL)),
                  _const_spec(wout.shape), tok],
        out_specs=tok,
        out_shape=jax.ShapeDtypeStruct((t, D_MODEL), F32),
        compiler_params=_params(1),
        name="pool_mix",
    )(z, z, z, wg, scale, wout, res)


def _top_rows(s, out_ref):
    cur = s
    for r in range(PEER_TOPK):
        m = jnp.max(cur, axis=0, keepdims=True)
        out_ref[r:r + 1, :] = m
        if r + 1 < PEER_TOPK:
            cur = jnp.where(cur == m, -jnp.inf, cur)


def _pair_sums(a1, a2):
    half = PEER_TOPK // 2
    slabs = [a1[0:1] + a2]
    slabs += [a1[i:i + 1] + a2[0:half] for i in range(1, half)]
    slabs += [a1[half:] + a2[0:1]]
    return jnp.concatenate(slabs, axis=0)


def _route_kernel(x_ref, g_ref, wqt_ref, k1_ref, k2_ref,
                  xnt_ref, s1z_ref, s2_ref, tau_ref,
                  qt_ref, a1_even_ref, a2_even_ref, a1_odd_ref, a2_odd_ref):
    xn = _rms(x_ref[...], g_ref[...])
    xnt = xn.T.astype(BF16)
    xnt_ref[0] = xnt
    q_half = qt_ref.shape[0] // 2
    qt_ref[:q_half] = jnp.dot(wqt_ref[:q_half], xnt, preferred_element_type=F32)
    qt_ref[q_half:] = jnp.dot(wqt_ref[q_half:], xnt, preferred_element_type=F32)

    def head(h, a1_ref, a2_ref):
        base = pl.multiple_of(h * PEER_QUERY_DIM, PEER_QUERY_DIM)
        q1 = qt_ref[pl.ds(base, PEER_HALF), :].astype(BF16)
        q2 = qt_ref[pl.ds(base + PEER_HALF, PEER_HALF), :].astype(BF16)
        s1z_ref[0, h] = jnp.dot(k1_ref[h], q1, preferred_element_type=F32) * LOG2_E
        s2_ref[0, h] = jnp.dot(k2_ref[h], q2, preferred_element_type=F32) * LOG2_E
        s1 = s1z_ref[0, h]
        s2 = s2_ref[0, h]
        _top_rows(s1, a1_ref)
        _top_rows(s2, a2_ref)
        a1 = a1_ref[...]
        a2 = a2_ref[...]
        m = a1[0:1] + a2[0:1]
        sums = _pair_sums(a1 - m, a2)
        cur = sums
        z = jnp.zeros_like(m)
        c = m
        for r in range(PEER_TOPK):
            c = jnp.max(cur, axis=0, keepdims=True)
            z = z + jnp.exp2(c)
            if r + 1 < PEER_TOPK:
                cur = jnp.where(cur == c, -jnp.inf, cur)
        shift = m + jnp.log2(z)
        s1z_ref[0, h] = s1 - shift
        gates_log2 = _pair_sums(a1 - shift, a2)
        tau_ref[0, pl.ds(h, 1), :] = jnp.min(jnp.where(sums >= c, gates_log2, jnp.inf), axis=0, keepdims=True)

    def head_pair(i, carry):
        head(2 * i, a1_even_ref, a2_even_ref)
        head(2 * i + 1, a1_odd_ref, a2_odd_ref)
        return carry

    lax.fori_loop(0, PEER_HEADS // 2, head_pair, 0)


def _peer_route(x, g, wqt, k1, k2):
    t = x.shape[0]
    sub = SUB_PEER
    n_sub = t // sub
    score_spec = pl.BlockSpec((1, PEER_HEADS, N_KEYS, sub), lambda i: (i, 0, 0, 0))
    return pl.pallas_call(
        _route_kernel,
        grid=(n_sub,),
        in_specs=[pl.BlockSpec((sub, D_MODEL), lambda i: (i, 0)), _const_spec((1, D_MODEL)),
                  _const_spec(wqt.shape), _const_spec(k1.shape), _const_spec(k2.shape)],
        out_specs=[pl.BlockSpec((1, D_MODEL, sub), lambda i: (i, 0, 0)), score_spec, score_spec,
                   pl.BlockSpec((1, PEER_HEADS, sub), lambda i: (i, 0, 0))],
        out_shape=[jax.ShapeDtypeStruct((n_sub, D_MODEL, sub), BF16),
                   jax.ShapeDtypeStruct((n_sub, PEER_HEADS, N_KEYS, sub), F32),
                   jax.ShapeDtypeStruct((n_sub, PEER_HEADS, N_KEYS, sub), F32),
                   jax.ShapeDtypeStruct((n_sub, PEER_HEADS, sub), F32)],
        scratch_shapes=[pltpu.VMEM((PEER_HEADS * PEER_QUERY_DIM, sub), F32)]
        + [pltpu.VMEM((PEER_TOPK, sub), F32)] * 4,
        compiler_params=_params(1),
        name="peer_route",
    )(x, g, wqt, k1, k2)


def _gelu(x):
    return x * 0.5 * (1.0 + lax.erf(x * (2.0 ** -0.5)))


def _zero_from(x):
    bits = lax.bitcast_convert_type(x, jnp.uint32)
    return lax.bitcast_convert_type((bits >> 16) >> 16, F32)


def _peer_kernel(xnt_ref, u_ref, vt_ref, s1z_ref, s2_ref, tau_ref, res_ref, fg_ref,
                 o_ref, acc_ref, st_even_ref, st_odd_ref, w_even_ref, w_odd_ref, *, final_norm):
    i = pl.program_id(0)
    j = pl.program_id(1)
    n_sub, _, sub_w = acc_ref.shape

    @pl.when(j == 0)
    def _():
        acc_ref[...] = jnp.zeros_like(acc_ref)

    @pl.when((i == 0) & (j == 0))
    def _():
        st_odd_ref[...] = jnp.zeros_like(st_odd_ref)
        w_odd_ref[...] = jnp.zeros_like(w_odd_ref)

    gate_block = jnp.clip(j - 1, 0, PEER_STEPS - 1)
    gate_valid = (j >= 1) & (j <= PEER_STEPS)

    def stages(st_new_ref, st_old_ref, w_new_ref, w_old_ref):
        zero = jnp.zeros((1, LANES), F32)
        for sub in range(n_sub):
            acc_ref[sub] += jnp.dot(vt_ref[...], w_old_ref[sub], preferred_element_type=F32)
            st_new_ref[sub] = jnp.dot(u_ref[...], xnt_ref[sub], preferred_element_type=F32)
            for kk in range(E1_PER_STEP):
                e1 = gate_block * E1_PER_STEP + kk
                s1z_rows = [s1z_ref[sub, h, pl.ds(e1, 1), :] for h in range(PEER_HEADS)]
                for lt in range(sub_w // LANES):
                    lanes = slice(lt * LANES, (lt + 1) * LANES)
                    for part in range(N_KEYS // GATE_ROWS):
                        keys = slice(part * GATE_ROWS, (part + 1) * GATE_ROWS)
                        rows = slice(kk * N_KEYS + part * GATE_ROWS, kk * N_KEYS + (part + 1) * GATE_ROWS)
                        gate = jnp.zeros((GATE_ROWS, LANES), F32)
                        for h in range(PEER_HEADS):
                            d = (s1z_rows[h][:, lanes] + zero) + s2_ref[sub, h, keys, lanes]
                            gate = gate + jnp.where(d >= tau_ref[sub, h:h + 1, lanes], jnp.exp2(d), 0.0)
                        w = gate * _gelu(st_old_ref[sub, rows, lanes])
                        w_new_ref[sub, rows, lanes] = jnp.where(gate_valid, w, 0.0).astype(BF16)
                        zero = _zero_from(w[GATE_ROWS - 1:GATE_ROWS, :])

    @pl.when(j % 2 == 0)
    def _():
        stages(st_even_ref, st_odd_ref, w_even_ref, w_odd_ref)

    @pl.when(j % 2 == 1)
    def _():
        stages(st_odd_ref, st_even_ref, w_odd_ref, w_even_ref)

    @pl.when(j == pl.num_programs(1) - 1)
    def _():
        for sub in range(n_sub):
            rows = slice(sub * sub_w, (sub + 1) * sub_w)
            y = res_ref[rows, :] + acc_ref[sub].T
            if final_norm:
                y = _rms(y, fg_ref[...])
            o_ref[rows, :] = y


def _peer_experts(xnt, u, vt, s1z, s2, tau, res, fg, final_norm):
    t = res.shape[0]
    tm = TM_PEER
    sub = SUB_PEER
    n_sub = tm // sub
    score_spec = pl.BlockSpec((n_sub, PEER_HEADS, N_KEYS, sub), lambda i, j: (i, 0, 0, 0))
    tok = pl.BlockSpec((tm, D_MODEL), lambda i, j: (i, 0))
    last = PEER_STEPS - 1
    return pl.pallas_call(
        functools.partial(_peer_kernel, final_norm=final_norm),
        grid=(t // tm, PEER_STEPS + 2),
        in_specs=[pl.BlockSpec((n_sub, D_MODEL, sub), lambda i, j: (i, 0, 0)),
                  pl.BlockSpec((EB_PEER, D_MODEL), lambda i, j: (jnp.minimum(j, last), 0)),
                  pl.BlockSpec((D_MODEL, EB_PEER), lambda i, j: (0, jnp.clip(j - 2, 0, last))),
                  score_spec, score_spec,
                  pl.BlockSpec((n_sub, PEER_HEADS, sub), lambda i, j: (i, 0, 0)), tok,
                  pl.BlockSpec((1, D_MODEL), lambda i, j: (0, 0))],
        out_specs=tok,
        out_shape=jax.ShapeDtypeStruct((t, D_MODEL), F32),
        scratch_shapes=[pltpu.VMEM((n_sub, D_MODEL, sub), F32),
                        pltpu.VMEM((n_sub, EB_PEER, sub), F32), pltpu.VMEM((n_sub, EB_PEER, sub), F32),
                        pltpu.VMEM((n_sub, EB_PEER, sub), BF16), pltpu.VMEM((n_sub, EB_PEER, sub), BF16)],
        compiler_params=_params(0, 2),
        name="peer_experts",
    )(xnt, u, vt, s1z, s2, tau, res, fg)


def _rope_tables(seq):
    rows = seq // GRID_W
    row = jnp.repeat(jnp.arange(rows, dtype=F32), GRID_W)
    col = jnp.tile(jnp.arange(GRID_W, dtype=F32), rows)
    inv = ROPE_THETA ** (-jnp.arange(0, AXIAL_DIM, 2, dtype=F32) / AXIAL_DIM)
    ang_r = row[:, None] * inv[None, :]
    ang_c = col[:, None] * inv[None, :]
    ang = jnp.concatenate([ang_r, ang_r, ang_c, ang_c], axis=-1)
    quarter = AXIAL_DIM // 2
    sign = jnp.where((jnp.arange(HEAD_DIM) % AXIAL_DIM) < quarter, -1.0, 1.0).astype(F32)
    return jnp.cos(ang), jnp.sin(ang) * sign[None, :]


def _peer(x, lw, final_gain, final_norm):
    xnt, s1z, s2, tau = _peer_route(x, lw["ffn_norm"], lw["wqt"], lw["k1"], lw["k2"])
    return _peer_experts(xnt, lw["u"], lw["vt"], s1z, s2, tau, x, final_gain, final_norm)


def _trunk(x3, w):
    batch, seq, _ = x3.shape
    x = x3.reshape(batch * seq, D_MODEL)
    cos, sin = _rope_tables(seq)
    q, k, v = _qkv_proj(x, w["attn_norm"], w["w_qkv"], w["q_gain"], w["k_gain"], cos, sin, seq)
    o = _attention(q, k, v, batch, seq)
    x = _proj_residual(o, w["w_o"], x)
    x = _peer(x, w["peer"][0], w["final_norm"], False)
    z = _norm_proj(x, w["pool_norm"], w["pool_w_in"])
    x = _pool_mix(z, w["pool_w_group"], w["pool_scale"], w["pool_w_out"], x, seq)
    x = _peer(x, w["peer"][1], w["final_norm"], True)
    return x.reshape(batch, seq, D_MODEL)


def kernel(x_prompt, x_sample, attn_norm, attn_w_qkv, attn_q_gain, attn_k_gain, attn_w_o,
           pool_norm, pool_w_in, pool_w_group, pool_scale, pool_w_out,
           ffn_norm, peer_w_query, peer_keys1, peer_keys2, peer_u, peer_v, final_norm):
    row = lambda a: a.reshape(1, -1)
    w = {
        "attn_norm": row(attn_norm[0]),
        "w_qkv": attn_w_qkv[0].astype(BF16),
        "q_gain": row(attn_q_gain[0]),
        "k_gain": row(attn_k_gain[0]),
        "w_o": attn_w_o[0].astype(BF16),
        "pool_norm": row(pool_norm[0]),
        "pool_w_in": pool_w_in[0].astype(BF16),
        "pool_w_group": pool_w_group[0].astype(BF16),
        "pool_scale": row(pool_scale[0]),
        "pool_w_out": pool_w_out[0].astype(BF16),
        "final_norm": row(final_norm),
        "peer": [
            {
                "ffn_norm": row(ffn_norm[i]),
                "wqt": peer_w_query[i].T.astype(BF16),
                "k1": peer_keys1[i].astype(BF16),
                "k2": peer_keys2[i].astype(BF16),
                "u": peer_u[i].astype(BF16),
                "vt": peer_v[i].T.astype(BF16),
            }
            for i in range(2)
        ],
    }
    return (_trunk(x_prompt, w), _trunk(x_sample, w))
```

```python
import functools

import jax
import jax.numpy as jnp
from jax import lax
from jax.experimental import pallas as pl
from jax.experimental.pallas import tpu as pltpu

F32 = jnp.float32
BF16 = jnp.bfloat16

D_MODEL = 2048
GRID_W = 64
N_HEADS = 16
N_KV_HEADS = 4
HEAD_DIM = D_MODEL // N_HEADS
GROUP = N_HEADS // N_KV_HEADS
Q_DIM = N_HEADS * HEAD_DIM
KV_DIM = N_KV_HEADS * HEAD_DIM
AXIAL_DIM = HEAD_DIM // 2
ROPE_THETA = 10000.0
POOL_WINDOWS = (2, 4, 8, 16)
POOL_GROUP_DIM = D_MODEL // len(POOL_WINDOWS)
PEER_HEADS = 8
PEER_QUERY_DIM = 256
PEER_HALF = PEER_QUERY_DIM // 2
N_KEYS = 128
N_EXPERTS = N_KEYS * N_KEYS
PEER_TOPK = 16
EPS = 1e-6

LOG2_E = 1.4426950408889634
LANES = 128
SUBLANES = 8
POOL_HALO = 8
VMEM_LIMIT = 56 << 20

TM_PROJ = 512
TQ_ATTN = 256
TK_ATTN = 2048
SUB_PEER = 512
TM_PEER = 512
E1_PER_STEP = 4
EB_PEER = E1_PER_STEP * N_KEYS
GATE_ROWS = 64
ACC_BLOCK_COLUMNS = 6
U_K_CHUNK = 512
PEER_STEPS = N_EXPERTS // EB_PEER

def _params(n_parallel, n_arbitrary=0):
    sem = ("parallel",) * n_parallel + ("arbitrary",) * n_arbitrary
    return pltpu.CompilerParams(dimension_semantics=sem, vmem_limit_bytes=VMEM_LIMIT)


def _const_spec(shape):
    nd = len(shape)
    return pl.BlockSpec(shape, lambda *_: (0,) * nd)


def _rms(x, g):
    return x * lax.rsqrt(jnp.mean(x * x, axis=-1, keepdims=True) + EPS) * g


def _qkv_kernel(x_ref, g_ref, w_ref, qg_ref, kg_ref, cos_ref, sin_ref, q_ref, k_ref, v_ref):
    tm = x_ref.shape[0]
    xn = _rms(x_ref[...], g_ref[...]).astype(BF16)
    cos = cos_ref[...]
    sin = sin_ref[...]
    lane = lax.broadcasted_iota(jnp.int32, (tm, HEAD_DIM), 1)
    first_quarter = (lane % AXIAL_DIM) < (AXIAL_DIM // 2)

    def head(y, gain, scale):
        y = _rms(y, gain)
        rot = jnp.where(first_quarter,
                        pltpu.roll(y, HEAD_DIM - AXIAL_DIM // 2, 1),
                        pltpu.roll(y, AXIAL_DIM // 2, 1))
        return (y * cos + rot * sin) * scale

    chunk = KV_DIM
    heads_per_chunk = chunk // HEAD_DIM
    q_scale = HEAD_DIM ** -0.5 * LOG2_E
    for c in range(Q_DIM // chunk):
        y = jnp.dot(xn, w_ref[:, c * chunk:(c + 1) * chunk], preferred_element_type=F32)
        for hh in range(heads_per_chunk):
            lo = hh * HEAD_DIM
            out = head(y[:, lo:lo + HEAD_DIM], qg_ref[...], q_scale)
            q_ref[:, c * chunk + lo:c * chunk + lo + HEAD_DIM] = out.astype(BF16)
    y = jnp.dot(xn, w_ref[:, Q_DIM:Q_DIM + KV_DIM], preferred_element_type=F32)
    for hh in range(heads_per_chunk):
        lo = hh * HEAD_DIM
        out = head(y[:, lo:lo + HEAD_DIM], kg_ref[...], 1.0)
        k_ref[:, lo:lo + HEAD_DIM] = out.astype(BF16)
    y = jnp.dot(xn, w_ref[:, Q_DIM + KV_DIM:], preferred_element_type=F32)
    ones_col = (lax.broadcasted_iota(jnp.int32, (tm, HEAD_DIM), 1) == 0).astype(BF16)
    for hh in range(heads_per_chunk):
        lo = hh * HEAD_DIM
        v_ref[:, 2 * lo:2 * lo + HEAD_DIM] = y[:, lo:lo + HEAD_DIM].astype(BF16)
        v_ref[:, 2 * lo + HEAD_DIM:2 * lo + 2 * HEAD_DIM] = ones_col


def _qkv_proj(x, g, w, qg, kg, cos, sin, seq):
    t = x.shape[0]
    tm = TM_PROJ
    tiles_per_seq = seq // tm
    tok = lambda n: pl.BlockSpec((tm, n), lambda i: (i, 0))
    pos = pl.BlockSpec((tm, HEAD_DIM), lambda i: (i % tiles_per_seq, 0))
    return pl.pallas_call(
        _qkv_kernel,
        grid=(t // tm,),
        in_specs=[tok(D_MODEL), _const_spec((1, D_MODEL)), _const_spec(w.shape),
                  _const_spec((1, HEAD_DIM)), _const_spec((1, HEAD_DIM)), pos, pos],
        out_specs=[tok(Q_DIM), tok(KV_DIM), tok(2 * KV_DIM)],
        out_shape=[jax.ShapeDtypeStruct((t, Q_DIM), BF16),
                   jax.ShapeDtypeStruct((t, KV_DIM), BF16),
                   jax.ShapeDtypeStruct((t, 2 * KV_DIM), BF16)],
        compiler_params=_params(1),
        name="qkv_proj",
    )(x, g, w, qg, kg, cos, sin)


def _attn_kernel(q_ref, k_ref, v_ref, o_ref):
    tq = q_ref.shape[0]
    seq = k_ref.shape[0]
    rows = GROUP * tq
    q = q_ref[...]
    qs = jnp.concatenate([q[:, h * HEAD_DIM:(h + 1) * HEAD_DIM] for h in range(GROUP)], axis=0)

    tk = min(TK_ATTN, seq)

    half_rows = rows // 2

    def body(c, carry):
        m, acc_top, acc_bot = carry
        start = pl.multiple_of(c * tk, tk)
        kc = k_ref[pl.ds(start, tk), :]
        vc = v_ref[pl.ds(start, tk), :]
        s = lax.dot_general(qs, kc, (((1,), (1,)), ((), ())), preferred_element_type=F32)
        m_new = jnp.maximum(m, jnp.max(s, axis=-1, keepdims=True))
        alpha = jnp.exp2(m - m_new)
        p = jnp.exp2(s - m_new).astype(BF16)
        acc_top = alpha[:half_rows] * acc_top + jnp.dot(p[:half_rows], vc, preferred_element_type=F32)
        acc_bot = alpha[half_rows:] * acc_bot + jnp.dot(p[half_rows:], vc, preferred_element_type=F32)
        return m_new, acc_top, acc_bot

    zeros_half = jnp.zeros((half_rows, 2 * HEAD_DIM), F32)
    init = (jnp.full((rows, 1), -jnp.inf, F32), zeros_half, zeros_half)
    _, acc_top, acc_bot = lax.fori_loop(0, seq // tk, body, init, unroll=True)
    acc = jnp.concatenate([acc_top, acc_bot], axis=0)
    o = acc[:, :HEAD_DIM] / acc[:, HEAD_DIM:HEAD_DIM + 1]
    for h in range(GROUP):
        o_ref[:, h * HEAD_DIM:(h + 1) * HEAD_DIM] = o[h * tq:(h + 1) * tq].astype(BF16)


def _attention(q, k, v, batch, seq):
    t = q.shape[0]
    tq = TQ_ATTN
    q_tiles = seq // tq
    qo_spec = pl.BlockSpec((tq, GROUP * HEAD_DIM), lambda b, g, i: (b * q_tiles + i, g))
    k_spec = pl.BlockSpec((seq, HEAD_DIM), lambda b, g, i: (b, g))
    v_spec = pl.BlockSpec((seq, 2 * HEAD_DIM), lambda b, g, i: (b, g))
    return pl.pallas_call(
        _attn_kernel,
        grid=(batch, N_KV_HEADS, q_tiles),
        in_specs=[qo_spec, k_spec, v_spec],
        out_specs=qo_spec,
        out_shape=jax.ShapeDtypeStruct((t, Q_DIM), BF16),
        compiler_params=_params(3),
        name="attention",
    )(q, k, v)


def _proj_residual_kernel(a_ref, w_ref, res_ref, o_ref):
    o_ref[...] = res_ref[...] + jnp.dot(a_ref[...], w_ref[...], preferred_element_type=F32)


def _proj_residual(a, w, res):
    t, k = a.shape
    n = w.shape[1]
    tm = TM_PROJ
    return pl.pallas_call(
        _proj_residual_kernel,
        grid=(t // tm,),
        in_specs=[pl.BlockSpec((tm, k), lambda i: (i, 0)), _const_spec(w.shape),
                  pl.BlockSpec((tm, n), lambda i: (i, 0))],
        out_specs=pl.BlockSpec((tm, n), lambda i: (i, 0)),
        out_shape=jax.ShapeDtypeStruct((t, n), F32),
        compiler_params=_params(1),
        name="proj_residual",
    )(a, w, res)


def _norm_proj_kernel(x_ref, g_ref, w_ref, o_ref):
    xn = _rms(x_ref[...], g_ref[...]).astype(BF16)
    o_ref[...] = jnp.dot(xn, w_ref[...], preferred_element_type=F32)


def _norm_proj(x, g, w):
    t, k = x.shape
    n = w.shape[1]
    tm = TM_PROJ
    return pl.pallas_call(
        _norm_proj_kernel,
        grid=(t // tm,),
        in_specs=[pl.BlockSpec((tm, k), lambda i: (i, 0)), _const_spec((1, k)), _const_spec(w.shape)],
        out_specs=pl.BlockSpec((tm, n), lambda i: (i, 0)),
        out_shape=jax.ShapeDtypeStruct((t, n), F32),
        compiler_params=_params(1),
        name="norm_proj",
    )(x, g, w)


def _pool_kernel(z_ref, zprev_ref, znext_ref, wg_ref, scale_ref, wout_ref, res_ref, o_ref, *, seq):
    tm = z_ref.shape[0]
    ext_rows = tm + 2 * POOL_HALO
    tiles_per_seq = seq // tm
    ti = pl.program_id(0) % tiles_per_seq
    has_prev = ti > 0
    has_next = ti < tiles_per_seq - 1
    pos = ti * tm + lax.broadcasted_iota(jnp.int32, (tm, 1), 0)
    mixed = []
    for g, w in enumerate(POOL_WINDOWS):
        sl = slice(g * POOL_GROUP_DIM, (g + 1) * POOL_GROUP_DIM)
        zc = z_ref[:, sl]
        zp = jnp.where(has_prev, zprev_ref[:, sl], 0.0)
        zn = jnp.where(has_next, znext_ref[:, sl], 0.0)
        a = jnp.concatenate([zp, zc, zn], axis=0)
        step = 1
        while step < w:
            a = a + pltpu.roll(a, ext_rows - step, 0)
            step *= 2
        if w // 2 < POOL_HALO:
            a = pltpu.roll(a, w // 2, 0)
            win = a[POOL_HALO:POOL_HALO + tm]
        else:
            win = a[0:tm]
        lo = jnp.maximum(pos - w // 2, 0)
        hi = jnp.minimum(pos + (w - w // 2), seq)
        cnt = (hi - lo).astype(F32)
        pooled = (win / cnt - zc).astype(BF16)
        mixed.append(jnp.dot(pooled, wg_ref[g], preferred_element_type=F32))
    m = jnp.concatenate(mixed, axis=1) * scale_ref[...]
    o_ref[...] = res_ref[...] + jnp.dot(m.astype(BF16), wout_ref[...], preferred_element_type=F32)


def _pool_mix(z, wg, scale, wout, res, seq):
    t = z.shape[0]
    tm = TM_PROJ
    halo_per_tile = tm // POOL_HALO
    n_halo_blocks = t // POOL_HALO
    tok = pl.BlockSpec((tm, D_MODEL), lambda i: (i, 0))
    prev = pl.BlockSpec((POOL_HALO, D_MODEL), lambda i: (jnp.maximum(i * halo_per_tile - 1, 0), 0))
    nxt = pl.BlockSpec((POOL_HALO, D_MODEL),
                       lambda i: (jnp.minimum((i + 1) * halo_per_tile, n_halo_blocks - 1), 0))
    return pl.pallas_call(
        functools.partial(_pool_kernel, seq=seq),
        grid=(t // tm,),
        in_specs=[tok, prev, nxt, _const_spec(wg.shape), _const_spec((1, D_MODEL)),
                  _const_spec(wout.shape), tok],
        out_specs=tok,
        out_shape=jax.ShapeDtypeStruct((t, D_MODEL), F32),
        compiler_params=_params(1),
        name="pool_mix",
    )(z, z, z, wg, scale, wout, res)


def _sorting_network(n):
    size = 1
    while size < n:
        size *= 2
    pairs = []

    def merge(lo, length, step):
        double = step * 2
        if double < length:
            merge(lo, length, double)
            merge(lo + step, length, double)
            for i in range(lo + step, lo + length - step, double):
                pairs.append((i, i + step))
        else:
            pairs.append((lo, lo + step))

    def sort(lo, length):
        if length > 1:
            mid = length // 2
            sort(lo, mid)
            sort(lo + mid, mid)
            merge(lo, length, 1)

    sort(0, size)
    return [(i, j) for i, j in pairs if j < n]


def _top_values(s, emit):
    tiles = [s[k:k + SUBLANES] for k in range(0, s.shape[0], SUBLANES)]
    for i, j in _sorting_network(len(tiles)):
        tiles[i], tiles[j] = jnp.maximum(tiles[i], tiles[j]), jnp.minimum(tiles[i], tiles[j])
    for r in range(PEER_TOPK):
        m = jnp.max(tiles[0], axis=0, keepdims=True)
        emit(r, m)
        levels = min(len(tiles), PEER_TOPK - r - 1)
        if levels == 0:
            break
        taken = tiles[0] == m
        tiles = [jnp.where(taken, tiles[k + 1] if k + 1 < len(tiles) else -jnp.inf, tiles[k])
                 for k in range(levels)]


def _top_rows(s, out_ref):
    def emit(r, m):
        out_ref[r:r + 1, :] = m
    _top_values(s, emit)


def _pair_sums(a1, a2):
    half = PEER_TOPK // 2
    slabs = [a1[0:1] + a2]
    slabs += [a1[i:i + 1] + a2[0:half] for i in range(1, half)]
    slabs += [a1[half:] + a2[0:1]]
    return jnp.concatenate(slabs, axis=0)


def _route_kernel(x_ref, x_next_ref, g_ref, wqt_ref, k1_ref, k2_ref,
                  xnt_ref, s1z_ref, s2_ref, tau_ref,
                  qt_even_ref, qt_odd_ref, xnt_next_ref, a1_even_ref, a2_even_ref, a1_odd_ref, a2_odd_ref):
    i = pl.program_id(0)

    def normalised_t(x):
        return _rms(x, g_ref[...]).T.astype(BF16)

    @pl.when(i == 0)
    def _():
        xnt = normalised_t(x_ref[...])
        xnt_next_ref[...] = xnt
        q_half = qt_even_ref.shape[0] // 2
        qt_even_ref[:q_half] = jnp.dot(wqt_ref[:q_half], xnt, preferred_element_type=F32)
        qt_even_ref[q_half:] = jnp.dot(wqt_ref[q_half:], xnt, preferred_element_type=F32)

    xnt_ref[0] = xnt_next_ref[...]
    xnt_next_ref[...] = normalised_t(x_next_ref[...])

    def scores(qt_ref):
        for h in range(PEER_HEADS):
            base = h * PEER_QUERY_DIM
            q1 = qt_ref[base:base + PEER_HALF, :].astype(BF16)
            q2 = qt_ref[base + PEER_HALF:base + PEER_QUERY_DIM, :].astype(BF16)
            s1z_ref[0, h] = jnp.dot(k1_ref[h], q1, preferred_element_type=F32) * LOG2_E
            s2_ref[0, h] = jnp.dot(k2_ref[h], q2, preferred_element_type=F32) * LOG2_E

    def head(h, a1_ref, a2_ref):
        s1 = s1z_ref[0, h]
        s2 = s2_ref[0, h]
        _top_rows(s1, a1_ref)
        _top_rows(s2, a2_ref)
        a1 = a1_ref[...]
        a2 = a2_ref[...]
        m = a1[0:1] + a2[0:1]
        sums = _pair_sums(a1 - m, a2)
        top = []
        _top_values(sums, lambda r, value: top.append(value))
        c = top[-1]
        z = sum(jnp.exp2(value) for value in top)
        shift = m + jnp.log2(z)
        s1z_ref[0, h] = s1 - shift
        gates_log2 = _pair_sums(a1 - shift, a2)
        tau_ref[0, pl.ds(h, 1), :] = jnp.min(jnp.where(sums >= c, gates_log2, jnp.inf), axis=0, keepdims=True)

    n_trips = PEER_HEADS // 2
    q_rows = qt_even_ref.shape[0] // n_trips

    def tile_loop(qt_ref, qt_next_ref):
        scores(qt_ref)

        def head_pair(p, carry):
            head(2 * p, a1_even_ref, a2_even_ref)
            head(2 * p + 1, a1_odd_ref, a2_odd_ref)
            r0 = pl.multiple_of(p * q_rows, q_rows)
            qt_next_ref[pl.ds(r0, q_rows), :] = jnp.dot(
                wqt_ref[pl.ds(r0, q_rows), :], xnt_next_ref[...], preferred_element_type=F32)
            return carry

        lax.fori_loop(0, n_trips, head_pair, 0)

    @pl.when(i % 2 == 0)
    def _():
        tile_loop(qt_even_ref, qt_odd_ref)

    @pl.when(i % 2 == 1)
    def _():
        tile_loop(qt_odd_ref, qt_even_ref)


def _peer_route(x, g, wqt, k1, k2):
    t = x.shape[0]
    sub = SUB_PEER
    n_sub = t // sub
    score_spec = pl.BlockSpec((1, PEER_HEADS, N_KEYS, sub), lambda i: (i, 0, 0, 0))
    return pl.pallas_call(
        _route_kernel,
        grid=(n_sub,),
        in_specs=[pl.BlockSpec((sub, D_MODEL), lambda i: (0, 0), pipeline_mode=pl.Buffered(1)),
                  pl.BlockSpec((sub, D_MODEL), lambda i: (jnp.minimum(i + 1, n_sub - 1), 0)),
                  _const_spec((1, D_MODEL)),
                  pl.BlockSpec(wqt.shape, lambda i: (0, 0), pipeline_mode=pl.Buffered(1)),
                  _const_spec(k1.shape), _const_spec(k2.shape)],
        out_specs=[pl.BlockSpec((1, D_MODEL, sub), lambda i: (i, 0, 0)), score_spec, score_spec,
                   pl.BlockSpec((1, PEER_HEADS, sub), lambda i: (i, 0, 0))],
        out_shape=[jax.ShapeDtypeStruct((n_sub, D_MODEL, sub), BF16),
                   jax.ShapeDtypeStruct((n_sub, PEER_HEADS, N_KEYS, sub), F32),
                   jax.ShapeDtypeStruct((n_sub, PEER_HEADS, N_KEYS, sub), F32),
                   jax.ShapeDtypeStruct((n_sub, PEER_HEADS, sub), F32)],
        scratch_shapes=[pltpu.VMEM((PEER_HEADS * PEER_QUERY_DIM, sub), F32)] * 2
        + [pltpu.VMEM((D_MODEL, sub), BF16)]
        + [pltpu.VMEM((PEER_TOPK, sub), F32)] * 4,
        compiler_params=_params(0, 1),
        name="peer_route",
    )(x, x, g, wqt, k1, k2)


def _gelu(x):
    return x * 0.5 * (1.0 + lax.erf(x * (2.0 ** -0.5)))


def _zero_from(x):
    bits = lax.bitcast_convert_type(x, jnp.uint32)
    return lax.bitcast_convert_type((bits >> 16) >> 16, F32)


def _peer_kernel(xnt_ref, u_ref, vt_ref, s1z_ref, s2_ref, tau_ref, res_ref, fg_ref,
                 o_ref, acc_ref, st_even_ref, st_odd_ref, w_even_ref, w_odd_ref, rows_ref, *, final_norm):
    t = pl.program_id(0)
    n_pairs = pl.num_programs(0) - 2
    n_sub, _, sub_w = acc_ref.shape
    acc_block = (t - 2) % PEER_STEPS

    @pl.when((t == 0) | (acc_block == 0))
    def _():
        acc_ref[...] = jnp.zeros_like(acc_ref)

    @pl.when(t == 0)
    def _():
        st_odd_ref[...] = jnp.zeros_like(st_odd_ref)
        w_odd_ref[...] = jnp.zeros_like(w_odd_ref)

    gate_block = jnp.maximum(t - 1, 0) % PEER_STEPS
    gate_valid = (t >= 1) & (t <= n_pairs)

    tau_row0 = E1_PER_STEP * PEER_HEADS
    for sub in range(n_sub):
        for kk in range(E1_PER_STEP):
            e1 = gate_block * E1_PER_STEP + kk
            for h in range(PEER_HEADS):
                r = kk * PEER_HEADS + h
                rows_ref[sub, r:r + 1, :] = s1z_ref[sub, h, pl.ds(e1, 1), :]
        rows_ref[sub, tau_row0:tau_row0 + PEER_HEADS, :] = jnp.where(gate_valid, tau_ref[sub], jnp.inf)

    def gate_tiles(st_old_ref, w_new_ref, columns):
        zero = jnp.zeros((1, LANES), F32)
        for sub in range(n_sub):
            for kk, lt in columns:
                lanes = slice(lt * LANES, (lt + 1) * LANES)
                for part in range(N_KEYS // GATE_ROWS):
                    keys = slice(part * GATE_ROWS, (part + 1) * GATE_ROWS)
                    rows = slice(kk * N_KEYS + part * GATE_ROWS, kk * N_KEYS + (part + 1) * GATE_ROWS)
                    gate = jnp.zeros((GATE_ROWS, LANES), F32)
                    for h in range(PEER_HEADS):
                        r = kk * PEER_HEADS + h
                        s1z_row = rows_ref[sub, r:r + 1, lanes]
                        tau_row = rows_ref[sub, tau_row0 + h:tau_row0 + h + 1, lanes]
                        d = (s1z_row + zero) + s2_ref[sub, h, keys, lanes]
                        gate = gate + jnp.where(d >= tau_row, jnp.exp2(d), 0.0)
                    w = gate * _gelu(st_old_ref[sub, rows, lanes])
                    w_new_ref[sub, rows, lanes] = w.astype(BF16)
                    zero = _zero_from(w[GATE_ROWS - 1:GATE_ROWS, :])

    def accumulate_stage(w_old_ref):
        for sub in range(n_sub):
            acc_ref[sub] += jnp.dot(vt_ref[0], w_old_ref[sub], preferred_element_type=F32)

    def score_stage(st_new_ref):
        for sub in range(n_sub):
            st_new_ref[sub] = sum(
                jnp.dot(u_ref[0, c], xnt_ref[sub, c * U_K_CHUNK:(c + 1) * U_K_CHUNK], preferred_element_type=F32)
                for c in range(D_MODEL // U_K_CHUNK))

    columns = [(kk, lt) for kk in range(E1_PER_STEP) for lt in range(sub_w // LANES)]
    first_keys = columns[:ACC_BLOCK_COLUMNS]
    last_keys = columns[ACC_BLOCK_COLUMNS:]
    even = t % 2 == 0
    odd = t % 2 == 1
    even_again = (t * 3) % 2 == 0
    odd_again = (t * 3) % 2 == 1

    @pl.when(even)
    def _():
        gate_tiles(st_odd_ref, w_even_ref, first_keys)
        accumulate_stage(w_odd_ref)

    @pl.when(even_again)
    def _():
        gate_tiles(st_odd_ref, w_even_ref, last_keys)
        score_stage(st_even_ref)

    @pl.when(odd)
    def _():
        gate_tiles(st_even_ref, w_odd_ref, first_keys)
        accumulate_stage(w_even_ref)

    @pl.when(odd_again)
    def _():
        gate_tiles(st_even_ref, w_odd_ref, last_keys)
        score_stage(st_odd_ref)

    @pl.when((t >= 2) & (acc_block == PEER_STEPS - 1))
    def _():
        for sub in range(n_sub):
            rows = slice(sub * sub_w, (sub + 1) * sub_w)
            y = res_ref[rows, :] + acc_ref[sub].T
            if final_norm:
                y = _rms(y, fg_ref[...])
            o_ref[rows, :] = y


def _peer_experts(xnt, u, vt, s1z, s2, tau, res, fg, final_norm):
    t = res.shape[0]
    tm = TM_PEER
    sub = SUB_PEER
    n_sub = tm // sub
    n_pairs = (t // tm) * PEER_STEPS

    def pair(step, lag):
        p = jnp.clip(step - lag, 0, n_pairs - 1)
        return p // PEER_STEPS, p % PEER_STEPS

    score_spec = pl.BlockSpec((n_sub, PEER_HEADS, N_KEYS, sub), lambda s: (pair(s, 1)[0], 0, 0, 0))
    tok = pl.BlockSpec((tm, D_MODEL), lambda s: (pair(s, 2)[0], 0))
    return pl.pallas_call(
        functools.partial(_peer_kernel, final_norm=final_norm),
        grid=(n_pairs + 2,),
        in_specs=[pl.BlockSpec((n_sub, D_MODEL, sub), lambda s: (pair(s, 0)[0], 0, 0)),
                  pl.BlockSpec((1, D_MODEL // U_K_CHUNK, EB_PEER, U_K_CHUNK), lambda s: (pair(s, 0)[1], 0, 0, 0)),
                  pl.BlockSpec((1, D_MODEL, EB_PEER), lambda s: (pair(s, 2)[1], 0, 0)),
                  score_spec, score_spec,
                  pl.BlockSpec((n_sub, PEER_HEADS, sub), lambda s: (pair(s, 1)[0], 0, 0)), tok,
                  pl.BlockSpec((1, D_MODEL), lambda s: (0, 0))],
        out_specs=tok,
        out_shape=jax.ShapeDtypeStruct((t, D_MODEL), F32),
        scratch_shapes=[pltpu.VMEM((n_sub, D_MODEL, sub), F32),
                        pltpu.VMEM((n_sub, EB_PEER, sub), F32), pltpu.VMEM((n_sub, EB_PEER, sub), F32),
                        pltpu.VMEM((n_sub, EB_PEER, sub), BF16), pltpu.VMEM((n_sub, EB_PEER, sub), BF16),
                        pltpu.VMEM((n_sub, (E1_PER_STEP + 1) * PEER_HEADS, sub), F32)],
        compiler_params=_params(0, 1),
        name="peer_experts",
    )(xnt, u, vt, s1z, s2, tau, res, fg)


def _rope_tables(seq):
    rows = seq // GRID_W
    row = jnp.repeat(jnp.arange(rows, dtype=F32), GRID_W)
    col = jnp.tile(jnp.arange(GRID_W, dtype=F32), rows)
    inv = ROPE_THETA ** (-jnp.arange(0, AXIAL_DIM, 2, dtype=F32) / AXIAL_DIM)
    ang_r = row[:, None] * inv[None, :]
    ang_c = col[:, None] * inv[None, :]
    ang = jnp.concatenate([ang_r, ang_r, ang_c, ang_c], axis=-1)
    quarter = AXIAL_DIM // 2
    sign = jnp.where((jnp.arange(HEAD_DIM) % AXIAL_DIM) < quarter, -1.0, 1.0).astype(F32)
    return jnp.cos(ang), jnp.sin(ang) * sign[None, :]


def _u_layout_kernel(u_ref, o_ref):
    for c in range(D_MODEL // U_K_CHUNK):
        o_ref[0, c] = u_ref[0, :, c * U_K_CHUNK:(c + 1) * U_K_CHUNK].astype(BF16)


def _v_layout_kernel(v_ref, o_ref):
    o_ref[0] = v_ref[0].T.astype(BF16)


def _expert_tables(u, v, layer):
    rows = pl.BlockSpec((1, EB_PEER, D_MODEL), lambda s: (layer, s, 0))
    slabs = D_MODEL // U_K_CHUNK
    u_out = pl.pallas_call(
        _u_layout_kernel,
        grid=(PEER_STEPS,),
        in_specs=[rows],
        out_specs=pl.BlockSpec((1, slabs, EB_PEER, U_K_CHUNK), lambda s: (s, 0, 0, 0)),
        out_shape=jax.ShapeDtypeStruct((PEER_STEPS, slabs, EB_PEER, U_K_CHUNK), BF16),
        compiler_params=_params(1),
        name="u_layout",
    )(u)
    vt_out = pl.pallas_call(
        _v_layout_kernel,
        grid=(PEER_STEPS,),
        in_specs=[rows],
        out_specs=pl.BlockSpec((1, D_MODEL, EB_PEER), lambda s: (s, 0, 0)),
        out_shape=jax.ShapeDtypeStruct((PEER_STEPS, D_MODEL, EB_PEER), BF16),
        compiler_params=_params(1),
        name="v_layout",
    )(v)
    return u_out, vt_out


def _peer(x, lw, final_gain, final_norm):
    xnt, s1z, s2, tau = _peer_route(x, lw["ffn_norm"], lw["wqt"], lw["k1"], lw["k2"])
    return _peer_experts(xnt, lw["u"], lw["vt"], s1z, s2, tau, x, final_gain, final_norm)


def _trunk(x3, w):
    batch, seq, _ = x3.shape
    x = x3.reshape(batch * seq, D_MODEL)
    cos, sin = _rope_tables(seq)
    q, k, v = _qkv_proj(x, w["attn_norm"], w["w_qkv"], w["q_gain"], w["k_gain"], cos, sin, seq)
    o = _attention(q, k, v, batch, seq)
    x = _proj_residual(o, w["w_o"], x)
    x = _peer(x, w["peer"][0], w["final_norm"], False)
    z = _norm_proj(x, w["pool_norm"], w["pool_w_in"])
    x = _pool_mix(z, w["pool_w_group"], w["pool_scale"], w["pool_w_out"], x, seq)
    x = _peer(x, w["peer"][1], w["final_norm"], True)
    return x.reshape(batch, seq, D_MODEL)


def kernel(x_prompt, x_sample, attn_norm, attn_w_qkv, attn_q_gain, attn_k_gain, attn_w_o,
           pool_norm, pool_w_in, pool_w_group, pool_scale, pool_w_out,
           ffn_norm, peer_w_query, peer_keys1, peer_keys2, peer_u, peer_v, final_norm):
    row = lambda a: a.reshape(1, -1)
    tables = [_expert_tables(peer_u, peer_v, i) for i in range(2)]
    w = {
        "attn_norm": row(attn_norm[0]),
        "w_qkv": attn_w_qkv[0].astype(BF16),
        "q_gain": row(attn_q_gain[0]),
        "k_gain": row(attn_k_gain[0]),
        "w_o": attn_w_o[0].astype(BF16),
        "pool_norm": row(pool_norm[0]),
        "pool_w_in": pool_w_in[0].astype(BF16),
        "pool_w_group": pool_w_group[0].astype(BF16),
        "pool_scale": row(pool_scale[0]),
        "pool_w_out": pool_w_out[0].astype(BF16),
        "final_norm": row(final_norm),
        "peer": [
            {
                "ffn_norm": row(ffn_norm[i]),
                "wqt": peer_w_query[i].T.astype(BF16),
                "k1": peer_keys1[i].astype(BF16),
                "k2": peer_keys2[i].astype(BF16),
                "u": tables[i][0],
                "vt": tables[i][1],
            }
            for i in range(2)
        ],
    }
    return (_trunk(x_prompt, w), _trunk(x_sample, w))
```

```python
import functools

import jax
import jax.numpy as jnp
from jax import lax
from jax.experimental import pallas as pl
from jax.experimental.pallas import tpu as pltpu

F32 = jnp.float32
BF16 = jnp.bfloat16

D_MODEL = 2048
GRID_W = 64
N_HEADS = 16
N_KV_HEADS = 4
HEAD_DIM = D_MODEL // N_HEADS
GROUP = N_HEADS // N_KV_HEADS
Q_DIM = N_HEADS * HEAD_DIM
KV_DIM = N_KV_HEADS * HEAD_DIM
AXIAL_DIM = HEAD_DIM // 2
ROPE_THETA = 10000.0
POOL_WINDOWS = (2, 4, 8, 16)
POOL_GROUP_DIM = D_MODEL // len(POOL_WINDOWS)
PEER_HEADS = 8
PEER_QUERY_DIM = 256
PEER_HALF = PEER_QUERY_DIM // 2
N_KEYS = 128
N_EXPERTS = N_KEYS * N_KEYS
PEER_TOPK = 16
EPS = 1e-6

LOG2_E = 1.4426950408889634
LANES = 128
SUBLANES = 8
POOL_HALO = 8
VMEM_LIMIT = 56 << 20

TM_PROJ = 512
TQ_ATTN = 512
TK_ATTN = 2048
SUB_PEER = 512
TM_PEER = 512
E1_PER_STEP = 4
EB_PEER = E1_PER_STEP * N_KEYS
PEER_STEPS = N_EXPERTS // EB_PEER
GATE_ROWS = 64
ACC_BLOCK_COLUMNS = 6
U_K_CHUNK = 512

def _params(n_parallel, n_arbitrary=0):
    sem = ("parallel",) * n_parallel + ("arbitrary",) * n_arbitrary
    return pltpu.CompilerParams(dimension_semantics=sem, vmem_limit_bytes=VMEM_LIMIT)


def _const_spec(shape):
    nd = len(shape)
    return pl.BlockSpec(shape, lambda *_: (0,) * nd)


def _rms(x, g):
    return x * lax.rsqrt(jnp.mean(x * x, axis=-1, keepdims=True) + EPS) * g


def _qkv_kernel(x_ref, g_ref, w_ref, qg_ref, kg_ref, cos_ref, sin_ref, q_ref, k_ref, v_ref):
    tm = x_ref.shape[0]
    xn = _rms(x_ref[...], g_ref[...]).astype(BF16)
    cos = cos_ref[...]
    sin = sin_ref[...]
    lane = lax.broadcasted_iota(jnp.int32, (tm, HEAD_DIM), 1)
    first_quarter = (lane % AXIAL_DIM) < (AXIAL_DIM // 2)

    def head(y, gain, scale):
        y = _rms(y, gain)
        rot = jnp.where(first_quarter,
                        pltpu.roll(y, HEAD_DIM - AXIAL_DIM // 2, 1),
                        pltpu.roll(y, AXIAL_DIM // 2, 1))
        return (y * cos + rot * sin) * scale

    chunk = KV_DIM
    heads_per_chunk = chunk // HEAD_DIM
    q_scale = HEAD_DIM ** -0.5 * LOG2_E
    for c in range(Q_DIM // chunk):
        y = jnp.dot(xn, w_ref[:, c * chunk:(c + 1) * chunk], preferred_element_type=F32)
        for hh in range(heads_per_chunk):
            lo = hh * HEAD_DIM
            out = head(y[:, lo:lo + HEAD_DIM], qg_ref[...], q_scale)
            q_ref[:, c * chunk + lo:c * chunk + lo + HEAD_DIM] = out.astype(BF16)
    y = jnp.dot(xn, w_ref[:, Q_DIM:Q_DIM + KV_DIM], preferred_element_type=F32)
    for hh in range(heads_per_chunk):
        lo = hh * HEAD_DIM
        out = head(y[:, lo:lo + HEAD_DIM], kg_ref[...], 1.0)
        k_ref[:, lo:lo + HEAD_DIM] = out.astype(BF16)
    y = jnp.dot(xn, w_ref[:, Q_DIM + KV_DIM:], preferred_element_type=F32)
    ones_col = (lax.broadcasted_iota(jnp.int32, (tm, HEAD_DIM), 1) == 0).astype(BF16)
    for hh in range(heads_per_chunk):
        lo = hh * HEAD_DIM
        v_ref[:, 2 * lo:2 * lo + HEAD_DIM] = y[:, lo:lo + HEAD_DIM].astype(BF16)
        v_ref[:, 2 * lo + HEAD_DIM:2 * lo + 2 * HEAD_DIM] = ones_col


def _qkv_proj(x, g, w, qg, kg, cos, sin, seq):
    t = x.shape[0]
    tm = TM_PROJ
    tiles_per_seq = seq // tm
    tok = lambda n: pl.BlockSpec((tm, n), lambda i: (i, 0))
    pos = pl.BlockSpec((tm, HEAD_DIM), lambda i: (i % tiles_per_seq, 0))
    return pl.pallas_call(
        _qkv_kernel,
        grid=(t // tm,),
        in_specs=[tok(D_MODEL), _const_spec((1, D_MODEL)), _const_spec(w.shape),
                  _const_spec((1, HEAD_DIM)), _const_spec((1, HEAD_DIM)), pos, pos],
        out_specs=[tok(Q_DIM), tok(KV_DIM), tok(2 * KV_DIM)],
        out_shape=[jax.ShapeDtypeStruct((t, Q_DIM), BF16),
                   jax.ShapeDtypeStruct((t, KV_DIM), BF16),
                   jax.ShapeDtypeStruct((t, 2 * KV_DIM), BF16)],
        compiler_params=_params(1),
        name="qkv_proj",
    )(x, g, w, qg, kg, cos, sin)


def _attn_kernel(q_ref, k_ref, v_ref, o_ref):
    tq = q_ref.shape[0]
    seq = k_ref.shape[0]
    rows = GROUP * tq
    q = q_ref[...]
    qs = jnp.concatenate([q[:, h * HEAD_DIM:(h + 1) * HEAD_DIM] for h in range(GROUP)], axis=0)

    tk = min(TK_ATTN, seq)

    half_rows = rows // 2

    def body(c, carry):
        m, acc_top, acc_bot = carry
        start = pl.multiple_of(c * tk, tk)
        kc = k_ref[pl.ds(start, tk), :]
        vc = v_ref[pl.ds(start, tk), :]
        s = lax.dot_general(qs, kc, (((1,), (1,)), ((), ())), preferred_element_type=F32)
        m_new = jnp.maximum(m, jnp.max(s, axis=-1, keepdims=True))
        alpha = jnp.exp2(m - m_new)
        p = jnp.exp2(s - m_new).astype(BF16)
        acc_top = alpha[:half_rows] * acc_top + jnp.dot(p[:half_rows], vc, preferred_element_type=F32)
        acc_bot = alpha[half_rows:] * acc_bot + jnp.dot(p[half_rows:], vc, preferred_element_type=F32)
        return m_new, acc_top, acc_bot

    zeros_half = jnp.zeros((half_rows, 2 * HEAD_DIM), F32)
    init = (jnp.full((rows, 1), -jnp.inf, F32), zeros_half, zeros_half)
    _, acc_top, acc_bot = lax.fori_loop(0, seq // tk, body, init, unroll=True)
    acc = jnp.concatenate([acc_top, acc_bot], axis=0)
    o = acc[:, :HEAD_DIM] / acc[:, HEAD_DIM:HEAD_DIM + 1]
    for h in range(GROUP):
        o_ref[:, h * HEAD_DIM:(h + 1) * HEAD_DIM] = o[h * tq:(h + 1) * tq].astype(BF16)


def _attention(q, k, v, batch, seq):
    t = q.shape[0]
    tq = TQ_ATTN
    q_tiles = seq // tq
    qo_spec = pl.BlockSpec((tq, GROUP * HEAD_DIM), lambda b, g, i: (b * q_tiles + i, g))
    k_spec = pl.BlockSpec((seq, HEAD_DIM), lambda b, g, i: (b, g))
    v_spec = pl.BlockSpec((seq, 2 * HEAD_DIM), lambda b, g, i: (b, g))
    return pl.pallas_call(
        _attn_kernel,
        grid=(batch, N_KV_HEADS, q_tiles),
        in_specs=[qo_spec, k_spec, v_spec],
        out_specs=qo_spec,
        out_shape=jax.ShapeDtypeStruct((t, Q_DIM), BF16),
        compiler_params=_params(3),
        name="attention",
    )(q, k, v)


def _proj_residual_kernel(a_ref, w_ref, res_ref, o_ref):
    o_ref[...] = res_ref[...] + jnp.dot(a_ref[...], w_ref[...], preferred_element_type=F32)


def _proj_residual(a, w, res):
    t, k = a.shape
    n = w.shape[1]
    tm = TM_PROJ
    return pl.pallas_call(
        _proj_residual_kernel,
        grid=(t // tm,),
        in_specs=[pl.BlockSpec((tm, k), lambda i: (i, 0)), _const_spec(w.shape),
                  pl.BlockSpec((tm, n), lambda i: (i, 0))],
        out_specs=pl.BlockSpec((tm, n), lambda i: (i, 0)),
        out_shape=jax.ShapeDtypeStruct((t, n), F32),
        compiler_params=_params(1),
        name="proj_residual",
    )(a, w, res)


def _norm_proj_kernel(x_ref, g_ref, w_ref, o_ref):
    xn = _rms(x_ref[...], g_ref[...]).astype(BF16)
    o_ref[...] = jnp.dot(xn, w_ref[...], preferred_element_type=F32)


def _norm_proj(x, g, w):
    t, k = x.shape
    n = w.shape[1]
    tm = TM_PROJ
    return pl.pallas_call(
        _norm_proj_kernel,
        grid=(t // tm,),
        in_specs=[pl.BlockSpec((tm, k), lambda i: (i, 0)), _const_spec((1, k)), _const_spec(w.shape)],
        out_specs=pl.BlockSpec((tm, n), lambda i: (i, 0)),
        out_shape=jax.ShapeDtypeStruct((t, n), F32),
        compiler_params=_params(1),
        name="norm_proj",
    )(x, g, w)


def _pool_kernel(z_ref, zprev_ref, znext_ref, wg_ref, scale_ref, wout_ref, res_ref, o_ref, *, seq):
    tm = z_ref.shape[0]
    ext_rows = tm + 2 * POOL_HALO
    tiles_per_seq = seq // tm
    ti = pl.program_id(0) % tiles_per_seq
    has_prev = ti > 0
    has_next = ti < tiles_per_seq - 1
    pos = ti * tm + lax.broadcasted_iota(jnp.int32, (tm, 1), 0)
    mixed = []
    for g, w in enumerate(POOL_WINDOWS):
        sl = slice(g * POOL_GROUP_DIM, (g + 1) * POOL_GROUP_DIM)
        zc = z_ref[:, sl]
        zp = jnp.where(has_prev, zprev_ref[:, sl], 0.0)
        zn = jnp.where(has_next, znext_ref[:, sl], 0.0)
        a = jnp.concatenate([zp, zc, zn], axis=0)
        step = 1
        while step < w:
            a = a + pltpu.roll(a, ext_rows - step, 0)
            step *= 2
        if w // 2 < POOL_HALO:
            a = pltpu.roll(a, w // 2, 0)
            win = a[POOL_HALO:POOL_HALO + tm]
        else:
            win = a[0:tm]
        lo = jnp.maximum(pos - w // 2, 0)
        hi = jnp.minimum(pos + (w - w // 2), seq)
        cnt = (hi - lo).astype(F32)
        pooled = (win / cnt - zc).astype(BF16)
        mixed.append(jnp.dot(pooled, wg_ref[g], preferred_element_type=F32))
    m = jnp.concatenate(mixed, axis=1) * scale_ref[...]
    o_ref[...] = res_ref[...] + jnp.dot(m.astype(BF16), wout_ref[...], preferred_element_type=F32)


def _pool_mix(z, wg, scale, wout, res, seq):
    t = z.shape[0]
    tm = TM_PROJ
    halo_per_tile = tm // POOL_HALO
    n_halo_blocks = t // POOL_HALO
    tok = pl.BlockSpec((tm, D_MODEL), lambda i: (i, 0))
    prev = pl.BlockSpec((POOL_HALO, D_MODEL), lambda i: (jnp.maximum(i * halo_per_tile - 1, 0), 0))
    nxt = pl.BlockSpec((POOL_HALO, D_MODEL),
                       lambda i: (jnp.minimum((i + 1) * halo_per_tile, n_halo_blocks - 1), 0))
    return pl.pallas_call(
        functools.partial(_pool_kernel, seq=seq),
        grid=(t // tm,),
        in_specs=[tok, prev, nxt, _const_spec(wg.shape), _const_spec((1, D_MODEL)),
                  _const_spec(wout.shape), tok],
        out_specs=tok,
        out_shape=jax.ShapeDtypeStruct((t, D_MODEL), F32),
        compiler_params=_params(1),
        name="pool_mix",
    )(z, z, z, wg, scale, wout, res)


def _sorting_network(n):
    size = 1
    while size < n:
        size *= 2
    pairs = []

    def merge(lo, length, step):
        double = step * 2
        if double < length:
            merge(lo, length, double)
            merge(lo + step, length, double)
            for i in range(lo + step, lo + length - step, double):
                pairs.append((i, i + step))
        else:
            pairs.append((lo, lo + step))

    def sort(lo, length):
        if length > 1:
            mid = length // 2
            sort(lo, mid)
            sort(lo + mid, mid)
            merge(lo, length, 1)

    sort(0, size)
    return [(i, j) for i, j in pairs if j < n]


def _top_values(s, emit):
    tiles = [s[k:k + SUBLANES] for k in range(0, s.shape[0], SUBLANES)]
    for i, j in _sorting_network(len(tiles)):
        tiles[i], tiles[j] = jnp.maximum(tiles[i], tiles[j]), jnp.minimum(tiles[i], tiles[j])
    for r in range(PEER_TOPK):
        m = jnp.max(tiles[0], axis=0, keepdims=True)
        emit(r, m)
        levels = min(len(tiles), PEER_TOPK - r - 1)
        if levels == 0:
            break
        taken = tiles[0] == m
        tiles = [jnp.where(taken, tiles[k + 1] if k + 1 < len(tiles) else -jnp.inf, tiles[k])
                 for k in range(levels)]


def _top_rows(s, out_ref):
    def emit(r, m):
        out_ref[r:r + 1, :] = m
    _top_values(s, emit)


def _pair_sums(a1, a2):
    half = PEER_TOPK // 2
    slabs = [a1[0:1] + a2]
    slabs += [a1[i:i + 1] + a2[0:half] for i in range(1, half)]
    slabs += [a1[half:] + a2[0:1]]
    return jnp.concatenate(slabs, axis=0)


def _route_kernel(x_ref, x_next_ref, g_ref, wqt_ref, k1_ref, k2_ref,
                  xnt_ref, s1z_ref, s2_ref, tau_ref,
                  qt_even_ref, qt_odd_ref, xnt_next_ref, a1_even_ref, a2_even_ref, a1_odd_ref, a2_odd_ref):
    i = pl.program_id(0)

    def normalised_t(x):
        return _rms(x, g_ref[...]).T.astype(BF16)

    @pl.when(i == 0)
    def _():
        xnt = normalised_t(x_ref[...])
        xnt_next_ref[...] = xnt
        q_half = qt_even_ref.shape[0] // 2
        qt_even_ref[:q_half] = jnp.dot(wqt_ref[:q_half], xnt, preferred_element_type=F32)
        qt_even_ref[q_half:] = jnp.dot(wqt_ref[q_half:], xnt, preferred_element_type=F32)

    xnt_ref[0] = xnt_next_ref[...]
    xnt_next_ref[...] = normalised_t(x_next_ref[...])

    def scores(qt_ref):
        for h in range(PEER_HEADS):
            base = h * PEER_QUERY_DIM
            q1 = qt_ref[base:base + PEER_HALF, :].astype(BF16)
            q2 = qt_ref[base + PEER_HALF:base + PEER_QUERY_DIM, :].astype(BF16)
            s1z_ref[0, h] = jnp.dot(k1_ref[h], q1, preferred_element_type=F32) * LOG2_E
            s2_ref[0, h] = jnp.dot(k2_ref[h], q2, preferred_element_type=F32) * LOG2_E

    def head(h, a1_ref, a2_ref):
        s1 = s1z_ref[0, h]
        s2 = s2_ref[0, h]
        _top_rows(s1, a1_ref)
        _top_rows(s2, a2_ref)
        a1 = a1_ref[...]
        a2 = a2_ref[...]
        m = a1[0:1] + a2[0:1]
        sums = _pair_sums(a1 - m, a2)
        top = []
        _top_values(sums, lambda r, value: top.append(value))
        c = top[-1]
        z = sum(jnp.exp2(value) for value in top)
        shift = m + jnp.log2(z)
        s1z_ref[0, h] = s1 - shift
        gates_log2 = _pair_sums(a1 - shift, a2)
        tau_ref[0, pl.ds(h, 1), :] = jnp.min(jnp.where(sums >= c, gates_log2, jnp.inf), axis=0, keepdims=True)

    n_trips = PEER_HEADS // 2
    q_rows = qt_even_ref.shape[0] // n_trips

    def tile_loop(qt_ref, qt_next_ref):
        scores(qt_ref)

        def head_pair(p, carry):
            head(2 * p, a1_even_ref, a2_even_ref)
            head(2 * p + 1, a1_odd_ref, a2_odd_ref)
            r0 = pl.multiple_of(p * q_rows, q_rows)
            qt_next_ref[pl.ds(r0, q_rows), :] = jnp.dot(
                wqt_ref[pl.ds(r0, q_rows), :], xnt_next_ref[...], preferred_element_type=F32)
            return carry

        lax.fori_loop(0, n_trips, head_pair, 0)

    @pl.when(i % 2 == 0)
    def _():
        tile_loop(qt_even_ref, qt_odd_ref)

    @pl.when(i % 2 == 1)
    def _():
        tile_loop(qt_odd_ref, qt_even_ref)


def _peer_route(x, g, wqt, k1, k2):
    t = x.shape[0]
    sub = SUB_PEER
    n_sub = t // sub
    score_spec = pl.BlockSpec((1, PEER_HEADS, N_KEYS, sub), lambda i: (i, 0, 0, 0))
    return pl.pallas_call(
        _route_kernel,
        grid=(n_sub,),
        in_specs=[pl.BlockSpec((sub, D_MODEL), lambda i: (0, 0), pipeline_mode=pl.Buffered(1)),
                  pl.BlockSpec((sub, D_MODEL), lambda i: (jnp.minimum(i + 1, n_sub - 1), 0)),
                  _const_spec((1, D_MODEL)),
                  pl.BlockSpec(wqt.shape, lambda i: (0, 0), pipeline_mode=pl.Buffered(1)),
                  _const_spec(k1.shape), _const_spec(k2.shape)],
        out_specs=[pl.BlockSpec((1, D_MODEL, sub), lambda i: (i, 0, 0)), score_spec, score_spec,
                   pl.BlockSpec((1, PEER_HEADS, sub), lambda i: (i, 0, 0))],
        out_shape=[jax.ShapeDtypeStruct((n_sub, D_MODEL, sub), BF16),
                   jax.ShapeDtypeStruct((n_sub, PEER_HEADS, N_KEYS, sub), F32),
                   jax.ShapeDtypeStruct((n_sub, PEER_HEADS, N_KEYS, sub), F32),
                   jax.ShapeDtypeStruct((n_sub, PEER_HEADS, sub), F32)],
        scratch_shapes=[pltpu.VMEM((PEER_HEADS * PEER_QUERY_DIM, sub), F32)] * 2
        + [pltpu.VMEM((D_MODEL, sub), BF16)]
        + [pltpu.VMEM((PEER_TOPK, sub), F32)] * 4,
        compiler_params=_params(0, 1),
        name="peer_route",
    )(x, x, g, wqt, k1, k2)


def _gelu(x):
    return x * 0.5 * (1.0 + lax.erf(x * (2.0 ** -0.5)))


def _zero_from(x):
    bits = lax.bitcast_convert_type(x, jnp.uint32)
    return lax.bitcast_convert_type((bits >> 16) >> 16, F32)


def _peer_kernel(xnt_ref, u_ref, vt_ref, s1z_ref, s2_ref, tau_ref, res_ref, fg_ref,
                 o_ref, acc_ref, st_even_ref, st_odd_ref, w_even_ref, w_odd_ref, rows_ref, *, final_norm):
    t = pl.program_id(0)
    n_pairs = pl.num_programs(0) - 2
    n_sub, _, sub_w = acc_ref.shape
    acc_block = (t - 2) % PEER_STEPS

    @pl.when((t == 0) | (acc_block == 0))
    def _():
        acc_ref[...] = jnp.zeros_like(acc_ref)

    @pl.when(t == 0)
    def _():
        st_odd_ref[...] = jnp.zeros_like(st_odd_ref)
        w_odd_ref[...] = jnp.zeros_like(w_odd_ref)

    gate_block = jnp.maximum(t - 1, 0) % PEER_STEPS
    gate_valid = (t >= 1) & (t <= n_pairs)

    tau_row0 = E1_PER_STEP * PEER_HEADS
    for sub in range(n_sub):
        for kk in range(E1_PER_STEP):
            e1 = gate_block * E1_PER_STEP + kk
            for h in range(PEER_HEADS):
                r = kk * PEER_HEADS + h
                rows_ref[sub, r:r + 1, :] = s1z_ref[sub, h, pl.ds(e1, 1), :]
        rows_ref[sub, tau_row0:tau_row0 + PEER_HEADS, :] = jnp.where(gate_valid, tau_ref[sub], jnp.inf)

    def gate_tiles(st_old_ref, w_new_ref, columns):
        zero = jnp.zeros((1, LANES), F32)
        for sub in range(n_sub):
            for kk, lt in columns:
                lanes = slice(lt * LANES, (lt + 1) * LANES)
                for part in range(N_KEYS // GATE_ROWS):
                    keys = slice(part * GATE_ROWS, (part + 1) * GATE_ROWS)
                    rows = slice(kk * N_KEYS + part * GATE_ROWS, kk * N_KEYS + (part + 1) * GATE_ROWS)
                    gate = jnp.zeros((GATE_ROWS, LANES), F32)
                    for h in range(PEER_HEADS):
                        r = kk * PEER_HEADS + h
                        s1z_row = rows_ref[sub, r:r + 1, lanes]
                        tau_row = rows_ref[sub, tau_row0 + h:tau_row0 + h + 1, lanes]
                        d = (s1z_row + zero) + s2_ref[sub, h, keys, lanes]
                        gate = gate + jnp.where(d >= tau_row, jnp.exp2(d), 0.0)
                    w = gate * _gelu(st_old_ref[sub, rows, lanes])
                    w_new_ref[sub, rows, lanes] = w.astype(BF16)
                    zero = _zero_from(w[GATE_ROWS - 1:GATE_ROWS, :])

    def accumulate_stage(w_old_ref):
        for sub in range(n_sub):
            acc_ref[sub] += jnp.dot(vt_ref[0], w_old_ref[sub], preferred_element_type=F32)

    def score_stage(st_new_ref):
        for sub in range(n_sub):
            st_new_ref[sub] = sum(
                jnp.dot(u_ref[0, c], xnt_ref[sub, c * U_K_CHUNK:(c + 1) * U_K_CHUNK], preferred_element_type=F32)
                for c in range(D_MODEL // U_K_CHUNK))

    columns = [(kk, lt) for kk in range(E1_PER_STEP) for lt in range(sub_w // LANES)]
    first_keys = columns[:ACC_BLOCK_COLUMNS]
    last_keys = columns[ACC_BLOCK_COLUMNS:]
    even = t % 2 == 0
    odd = t % 2 == 1
    even_again = (t * 3) % 2 == 0
    odd_again = (t * 3) % 2 == 1

    @pl.when(even)
    def _():
        gate_tiles(st_odd_ref, w_even_ref, first_keys)
        accumulate_stage(w_odd_ref)

    @pl.when(even_again)
    def _():
        gate_tiles(st_odd_ref, w_even_ref, last_keys)
        score_stage(st_even_ref)

    @pl.when(odd)
    def _():
        gate_tiles(st_even_ref, w_odd_ref, first_keys)
        accumulate_stage(w_even_ref)

    @pl.when(odd_again)
    def _():
        gate_tiles(st_even_ref, w_odd_ref, last_keys)
        score_stage(st_odd_ref)

    @pl.when((t >= 2) & (acc_block == PEER_STEPS - 1))
    def _():
        for sub in range(n_sub):
            rows = slice(sub * sub_w, (sub + 1) * sub_w)
            y = res_ref[rows, :] + acc_ref[sub].T
            if final_norm:
                y = _rms(y, fg_ref[...])
            o_ref[rows, :] = y


def _peer_experts(xnt, u, vt, s1z, s2, tau, res, fg, final_norm):
    t = res.shape[0]
    tm = TM_PEER
    sub = SUB_PEER
    n_sub = tm // sub
    n_pairs = (t // tm) * PEER_STEPS

    def pair(step, lag):
        p = jnp.clip(step - lag, 0, n_pairs - 1)
        return p // PEER_STEPS, p % PEER_STEPS

    score_spec = pl.BlockSpec((n_sub, PEER_HEADS, N_KEYS, sub), lambda s: (pair(s, 1)[0], 0, 0, 0))
    tok = pl.BlockSpec((tm, D_MODEL), lambda s: (pair(s, 2)[0], 0))
    return pl.pallas_call(
        functools.partial(_peer_kernel, final_norm=final_norm),
        grid=(n_pairs + 2,),
        in_specs=[pl.BlockSpec((n_sub, D_MODEL, sub), lambda s: (pair(s, 0)[0], 0, 0)),
                  pl.BlockSpec((1, D_MODEL // U_K_CHUNK, EB_PEER, U_K_CHUNK), lambda s: (pair(s, 0)[1], 0, 0, 0)),
                  pl.BlockSpec((1, D_MODEL, EB_PEER), lambda s: (pair(s, 2)[1], 0, 0)),
                  score_spec, score_spec,
                  pl.BlockSpec((n_sub, PEER_HEADS, sub), lambda s: (pair(s, 1)[0], 0, 0)), tok,
                  pl.BlockSpec((1, D_MODEL), lambda s: (0, 0))],
        out_specs=tok,
        out_shape=jax.ShapeDtypeStruct((t, D_MODEL), F32),
        scratch_shapes=[pltpu.VMEM((n_sub, D_MODEL, sub), F32),
                        pltpu.VMEM((n_sub, EB_PEER, sub), F32), pltpu.VMEM((n_sub, EB_PEER, sub), F32),
                        pltpu.VMEM((n_sub, EB_PEER, sub), BF16), pltpu.VMEM((n_sub, EB_PEER, sub), BF16),
                        pltpu.VMEM((n_sub, (E1_PER_STEP + 1) * PEER_HEADS, sub), F32)],
        compiler_params=_params(0, 1),
        name="peer_experts",
    )(xnt, u, vt, s1z, s2, tau, res, fg)


def _rope_tables(seq):
    rows = seq // GRID_W
    row = jnp.repeat(jnp.arange(rows, dtype=F32), GRID_W)
    col = jnp.tile(jnp.arange(GRID_W, dtype=F32), rows)
    inv = ROPE_THETA ** (-jnp.arange(0, AXIAL_DIM, 2, dtype=F32) / AXIAL_DIM)
    ang_r = row[:, None] * inv[None, :]
    ang_c = col[:, None] * inv[None, :]
    ang = jnp.concatenate([ang_r, ang_r, ang_c, ang_c], axis=-1)
    quarter = AXIAL_DIM // 2
    sign = jnp.where((jnp.arange(HEAD_DIM) % AXIAL_DIM) < quarter, -1.0, 1.0).astype(F32)
    return jnp.cos(ang), jnp.sin(ang) * sign[None, :]


def _u_layout_kernel(u_ref, o_ref):
    for c in range(D_MODEL // U_K_CHUNK):
        o_ref[0, c] = u_ref[0, :, c * U_K_CHUNK:(c + 1) * U_K_CHUNK].astype(BF16)


def _v_layout_kernel(v_ref, o_ref):
    o_ref[0] = v_ref[0].T.astype(BF16)


def _expert_tables(u, v, layer):
    rows = pl.BlockSpec((1, EB_PEER, D_MODEL), lambda s: (layer, s, 0))
    slabs = D_MODEL // U_K_CHUNK
    u_out = pl.pallas_call(
        _u_layout_kernel,
        grid=(PEER_STEPS,),
        in_specs=[rows],
        out_specs=pl.BlockSpec((1, slabs, EB_PEER, U_K_CHUNK), lambda s: (s, 0, 0, 0)),
        out_shape=jax.ShapeDtypeStruct((PEER_STEPS, slabs, EB_PEER, U_K_CHUNK), BF16),
        compiler_params=_params(1),
        name="u_layout",
    )(u)
    vt_out = pl.pallas_call(
        _v_layout_kernel,
        grid=(PEER_STEPS,),
        in_specs=[rows],
        out_specs=pl.BlockSpec((1, D_MODEL, EB_PEER), lambda s: (s, 0, 0)),
        out_shape=jax.ShapeDtypeStruct((PEER_STEPS, D_MODEL, EB_PEER), BF16),
        compiler_params=_params(1),
        name="v_layout",
    )(v)
    return u_out, vt_out


def _peer(x, lw, final_gain, final_norm):
    xnt, s1z, s2, tau = _peer_route(x, lw["ffn_norm"], lw["wqt"], lw["k1"], lw["k2"])
    return _peer_experts(xnt, lw["u"], lw["vt"], s1z, s2, tau, x, final_gain, final_norm)


def _trunk(x3, w):
    batch, seq, _ = x3.shape
    x = x3.reshape(batch * seq, D_MODEL)
    cos, sin = _rope_tables(seq)
    q, k, v = _qkv_proj(x, w["attn_norm"], w["w_qkv"], w["q_gain"], w["k_gain"], cos, sin, seq)
    o = _attention(q, k, v, batch, seq)
    x = _proj_residual(o, w["w_o"], x)
    x = _peer(x, w["peer"][0], w["final_norm"], False)
    z = _norm_proj(x, w["pool_norm"], w["pool_w_in"])
    x = _pool_mix(z, w["pool_w_group"], w["pool_scale"], w["pool_w_out"], x, seq)
    x = _peer(x, w["peer"][1], w["final_norm"], True)
    return x.reshape(batch, seq, D_MODEL)


def kernel(x_prompt, x_sample, attn_norm, attn_w_qkv, attn_q_gain, attn_k_gain, attn_w_o,
           pool_norm, pool_w_in, pool_w_group, pool_scale, pool_w_out,
           ffn_norm, peer_w_query, peer_keys1, peer_keys2, peer_u, peer_v, final_norm):
    row = lambda a: a.reshape(1, -1)
    tables = [_expert_tables(peer_u, peer_v, i) for i in range(2)]
    w = {
        "attn_norm": row(attn_norm[0]),
        "w_qkv": attn_w_qkv[0].astype(BF16),
        "q_gain": row(attn_q_gain[0]),
        "k_gain": row(attn_k_gain[0]),
        "w_o": attn_w_o[0].astype(BF16),
        "pool_norm": row(pool_norm[0]),
        "pool_w_in": pool_w_in[0].astype(BF16),
        "pool_w_group": pool_w_group[0].astype(BF16),
        "pool_scale": row(pool_scale[0]),
        "pool_w_out": pool_w_out[0].astype(BF16),
        "final_norm": row(final_norm),
        "peer": [
            {
                "ffn_norm": row(ffn_norm[i]),
                "wqt": peer_w_query[i].T.astype(BF16),
                "k1": peer_keys1[i].astype(BF16),
                "k2": peer_keys2[i].astype(BF16),
                "u": tables[i][0],
                "vt": tables[i][1],
            }
            for i in range(2)
        ],
    }
    return (_trunk(x_prompt, w), _trunk(x_sample, w))
```
